```python
import math
import jax, jax.numpy as jnp
from jax import lax
import numpy as np

D_MODEL = 1024
BATCH = 4
SEQ = 4096
DEPTH = 4
DEC_BATCH = 128
DEC_SEQ = 1
PAST_LEN = 2048
PAGE_SIZE = 128

N_MIXERS = 3
N_CONV_LAYERS = (DEPTH + 2) // 3
N_ATTN_LAYERS = (DEPTH + 1) // 3
N_SSM_LAYERS = DEPTH // 3
N_SUB = 3
N_HEADS = 16
HEAD_DIM = D_MODEL // N_HEADS
SB_SCALE = HEAD_DIM ** -0.5
SB_BIAS_INIT = -6.0
Q_BLOCK = 128
CONV_WIDTH = 3
SSM_GROUP = 16
SSM_GROUPS = D_MODEL // SSM_GROUP
SSM_STATE = 64
LAMBDA_RE_MAX = -1e-4
D_FF = ((8 * D_MODEL // 3 + 127) // 128) * 128
FFN_RES_WEIGHT = 0.5
RMS_EPS = 1e-6

kernel_name = "hybrid_conv_stickbreak_s5_macaron_step"


def rms_norm(x, gain):
    xf = x.astype(jnp.float32)
    y = xf * lax.rsqrt(jnp.mean(xf * xf, axis=-1, keepdims=True) + RMS_EPS)
    return (y * gain.astype(jnp.float32)).astype(x.dtype)


def modulated_norm(x, gain, shift, scale):
    return rms_norm(x, gain) * (1.0 + scale[:, None, :]) + shift[:, None, :]


def swiglu(h, w13, w2):
    g, u = jnp.split(h @ w13, 2, axis=-1)
    return (jax.nn.silu(g) * u) @ w2


def short_conv_mixer(h, conv_prev, w_in, conv_w, w_out):
    L = h.shape[1]
    b_gate, c_gate, xin = jnp.split(h @ w_in, 3, axis=-1)
    u = c_gate * xin
    u_ext = jnp.concatenate([conv_prev.astype(u.dtype), u], axis=1)
    conv = u_ext[:, 0:L] * conv_w[0]
    for j in range(1, CONV_WIDTH):
        conv = conv + u_ext[:, j:j + L] * conv_w[j]
    y = (b_gate * conv) @ w_out
    return y, u_ext[:, u_ext.shape[1] - (CONV_WIDTH - 1):]


def qkv_project(h, w_qkv, q_gain, k_gain):
    b, L, _ = h.shape
    qkv = (h @ w_qkv).reshape(b, L, 3, N_HEADS, HEAD_DIM)
    q = rms_norm(qkv[:, :, 0], q_gain)
    k = rms_norm(qkv[:, :, 1], k_gain)
    return q, k, qkv[:, :, 2]


def stick_breaking_weights(z, q_pos, k_pos):
    mask = k_pos[None, :] < q_pos[:, None]
    log_beta = jax.nn.log_sigmoid(z)
    log_1m_beta = jnp.where(mask, jax.nn.log_sigmoid(-z), 0.0)
    log_stick = lax.cumsum(log_1m_beta, axis=3, reverse=True) - log_1m_beta
    return jnp.where(mask, jnp.exp(log_beta + log_stick), 0.0)


def sb_attention_prompt(q, k, v, bias):
    b, L = q.shape[0], q.shape[1]
    nb = L // Q_BLOCK
    pos = jnp.arange(L)
    q_blocks = q.reshape(b, nb, Q_BLOCK, N_HEADS, HEAD_DIM).transpose(1, 0, 2, 3, 4)
    pos_blocks = pos.reshape(nb, Q_BLOCK)
    bias_f = bias.astype(jnp.float32)[None, :, None, None]

    def one_block(args):
        q_blk, q_pos = args
        z = jnp.einsum("bqhd,bkhd->bhqk", q_blk, k).astype(jnp.float32) * SB_SCALE + bias_f
        w = stick_breaking_weights(z, q_pos, pos)
        return jnp.einsum("bhqk,bkhd->bqhd", w.astype(v.dtype), v)

    o = lax.map(one_block, (q_blocks, pos_blocks))
    return o.transpose(1, 0, 2, 3, 4).reshape(b, L, D_MODEL)


def sb_attention_sample(q, k, v, cache_k_l, cache_v_l, page_table, bias):
    b, S = q.shape[0], q.shape[1]
    n_past = page_table.shape[1] * PAGE_SIZE
    past_k = cache_k_l[page_table].reshape(b, n_past, N_HEADS, HEAD_DIM)
    past_v = cache_v_l[page_table].reshape(b, n_past, N_HEADS, HEAD_DIM)
    bias_f = bias.astype(jnp.float32)[None, :, None, None]
    z = jnp.concatenate([
        jnp.einsum("bqhd,bkhd->bhqk", q, past_k.astype(q.dtype)).astype(jnp.float32),
        jnp.einsum("bqhd,bkhd->bhqk", q, k).astype(jnp.float32)], axis=3) * SB_SCALE + bias_f
    q_pos = n_past + jnp.arange(S)
    k_pos = jnp.arange(n_past + S)
    w = stick_breaking_weights(z, q_pos, k_pos).astype(v.dtype)
    o = (jnp.einsum("bhqk,bkhd->bqhd", w[..., :n_past], past_v.astype(v.dtype))
         + jnp.einsum("bhqk,bkhd->bqhd", w[..., n_past:], v))
    return o.reshape(b, S, D_MODEL)


def s5_mixer(h, s_prev_re, s_prev_im, lam_re, lam_im, log_dt, b_re, b_im, c_re, c_im,
             d_skip, w_glu, b_glu):
    f32 = jnp.float32
    bsz, L, _ = h.shape
    lam_re = jnp.minimum(lam_re.astype(f32), LAMBDA_RE_MAX)
    lam_im = lam_im.astype(f32)
    dt = jnp.exp(log_dt.astype(f32))[:, None]
    decay = jnp.exp(lam_re * dt)
    a_re = decay * jnp.cos(lam_im * dt)
    a_im = decay * jnp.sin(lam_im * dt)
    inv = 1.0 / (lam_re * lam_re + lam_im * lam_im)
    f_re = ((a_re - 1.0) * lam_re + a_im * lam_im) * inv
    f_im = (a_im * lam_re - (a_re - 1.0) * lam_im) * inv
    b_re, b_im = b_re.astype(f32), b_im.astype(f32)
    bb_re = f_re[..., None] * b_re - f_im[..., None] * b_im
    bb_im = f_re[..., None] * b_im + f_im[..., None] * b_re
    u_flat = h.astype(f32)
    u = u_flat.reshape(bsz, L, SSM_GROUPS, SSM_GROUP)
    bu_re = jnp.einsum("gpc,blgc->blgp", bb_re, u)
    bu_im = jnp.einsum("gpc,blgc->blgp", bb_im, u)
    s_prev_re, s_prev_im = s_prev_re.astype(f32), s_prev_im.astype(f32)
    bu_re = bu_re.at[:, 0].add(a_re * s_prev_re - a_im * s_prev_im)
    bu_im = bu_im.at[:, 0].add(a_re * s_prev_im + a_im * s_prev_re)
    ar = jnp.broadcast_to(a_re, bu_re.shape)
    ai = jnp.broadcast_to(a_im, bu_im.shape)

    def combine(e1, e2):
        a1r, a1i, b1r, b1i = e1
        a2r, a2i, b2r, b2i = e2
        return (a1r * a2r - a1i * a2i, a1r * a2i + a1i * a2r,
                a2r * b1r - a2i * b1i + b2r, a2r * b1i + a2i * b1r + b2i)

    _, _, s_re, s_im = lax.associative_scan(combine, (ar, ai, bu_re, bu_im), axis=1)
    y = (jnp.einsum("gcp,blgp->blgc", c_re.astype(f32), s_re)
         - jnp.einsum("gcp,blgp->blgc", c_im.astype(f32), s_im))
    y = (y.reshape(bsz, L, D_MODEL) + d_skip.astype(f32) * u_flat).astype(h.dtype)
    ga, gb = jnp.split(y @ w_glu + b_glu, 2, axis=-1)
    return ga * jax.nn.sigmoid(gb), s_re[:, L - 1], s_im[:, L - 1]


def trunk(x, c, conv_prev, ssm_prev_re, ssm_prev_im, attend, p):
    bsz = x.shape[0]
    c_act = jax.nn.silu(c)
    conv_new, k_new, v_new, ssm_new_re, ssm_new_im = [], [], [], [], []
    for i in range(DEPTH):
        kind, j = i % N_MIXERS, i // N_MIXERS
        mod = (c_act @ p["ada_w"][i] + p["ada_b"][i]).reshape(bsz, N_SUB, 3, D_MODEL)
        shift, scale, gate = mod[:, :, 0], mod[:, :, 1], mod[:, :, 2]
        h = modulated_norm(x, p["ln_gain"][i, 0], shift[:, 0], scale[:, 0])
        x = x + FFN_RES_WEIGHT * gate[:, 0, None] * swiglu(h, p["ffn_w13"][i, 0], p["ffn_w2"][i, 0])
        h = modulated_norm(x, p["ln_gain"][i, 1], shift[:, 1], scale[:, 1])
        if kind == 0:
            m, st = short_conv_mixer(h, conv_prev[j], p["conv_w_in"][j], p["conv_w"][j], p["conv_w_out"][j])
            conv_new.append(st)
        elif kind == 1:
            q, k, v = qkv_project(h, p["attn_w_qkv"][j], p["attn_q_gain"][j], p["attn_k_gain"][j])
            m = attend(j, q, k, v) @ p["attn_w_o"][j]
            k_new.append(k)
            v_new.append(v)
        else:
            m, sr, si = s5_mixer(h, ssm_prev_re[j], ssm_prev_im[j], p["ssm_lambda_re"][j],
                                 p["ssm_lambda_im"][j], p["ssm_log_dt"][j], p["ssm_b_re"][j],
                                 p["ssm_b_im"][j], p["ssm_c_re"][j], p["ssm_c_im"][j], p["ssm_d"][j],
                                 p["ssm_w_glu"][j], p["ssm_b_glu"][j])
            ssm_new_re.append(sr)
            ssm_new_im.append(si)
        x = x + gate[:, 1, None] * m
        h = modulated_norm(x, p["ln_gain"][i, 2], shift[:, 2], scale[:, 2])
        x = x + FFN_RES_WEIGHT * gate[:, 2, None] * swiglu(h, p["ffn_w13"][i, 1], p["ffn_w2"][i, 1])
    return (x, jnp.stack(conv_new), jnp.stack(k_new), jnp.stack(v_new),
            jnp.stack(ssm_new_re), jnp.stack(ssm_new_im))


def setup_inputs(seed: int = 0) -> dict:
    key = jax.random.key(seed)
    ks = iter(list(jax.random.split(key, 40)))
    f32 = jnp.float32

    def nrm(shape, s):
        return jax.random.normal(next(ks), shape, f32) * s

    n_pages = PAST_LEN // PAGE_SIZE
    n_phys = (5 * DEC_BATCH * n_pages + 3) // 4
    page_table = jax.random.permutation(next(ks), n_phys)[:DEC_BATCH * n_pages]
    page_table = page_table.reshape(DEC_BATCH, n_pages).astype(jnp.int32)
    d = D_MODEL
    lam_im_base = jnp.pi * jnp.arange(SSM_STATE, dtype=f32)
    return {
        "x_prompt": nrm((BATCH, SEQ, d), 1.0),
        "x_sample": nrm((DEC_BATCH, DEC_SEQ, d), 1.0),
        "state_conv": nrm((N_CONV_LAYERS, DEC_BATCH, CONV_WIDTH - 1, d), 0.5),
        "cache_k": nrm((N_ATTN_LAYERS, n_phys, PAGE_SIZE, N_HEADS, HEAD_DIM), 1.0),
        "cache_v": nrm((N_ATTN_LAYERS, n_phys, PAGE_SIZE, N_HEADS, HEAD_DIM), 1.0),
        "state_ssm_re": nrm((N_SSM_LAYERS, DEC_BATCH, SSM_GROUPS, SSM_STATE), 0.3),
        "state_ssm_im": nrm((N_SSM_LAYERS, DEC_BATCH, SSM_GROUPS, SSM_STATE), 0.3),
        "page_table": page_table,
        "c_prompt": nrm((BATCH, d), 1.0),
        "c_sample": nrm((DEC_BATCH, d), 1.0),
        "ln_gain": 1.0 + nrm((DEPTH, N_SUB, d), 0.02),
        "ada_w": nrm((DEPTH, d, N_SUB * 3 * d), 0.5 * d ** -0.5),
        "ada_b": nrm((DEPTH, N_SUB * 3 * d), 0.02),
        "ffn_w13": nrm((DEPTH, 2, d, 2 * D_FF), d ** -0.5),
        "ffn_w2": nrm((DEPTH, 2, D_FF, d), D_FF ** -0.5),
        "conv_w_in": nrm((N_CONV_LAYERS, d, 3 * d), d ** -0.5),
        "conv_w": nrm((N_CONV_LAYERS, CONV_WIDTH, d), CONV_WIDTH ** -0.5),
        "conv_w_out": nrm((N_CONV_LAYERS, d, d), d ** -0.5),
        "attn_w_qkv": nrm((N_ATTN_LAYERS, d, 3 * d), d ** -0.5),
        "attn_q_gain": 1.0 + nrm((N_ATTN_LAYERS, HEAD_DIM), 0.02),
        "attn_k_gain": 1.0 + nrm((N_ATTN_LAYERS, HEAD_DIM), 0.02),
        "attn_logit_bias": SB_BIAS_INIT + nrm((N_ATTN_LAYERS, N_HEADS), 0.1),
        "attn_w_o": nrm((N_ATTN_LAYERS, d, d), d ** -0.5),
        "ssm_lambda_re": -0.5 + nrm((N_SSM_LAYERS, SSM_GROUPS, SSM_STATE), 0.01),
        "ssm_lambda_im": lam_im_base + nrm((N_SSM_LAYERS, SSM_GROUPS, SSM_STATE), 0.01),
        "ssm_log_dt": jax.random.uniform(next(ks), (N_SSM_LAYERS, SSM_GROUPS), f32,
                                         minval=math.log(1e-3), maxval=math.log(1e-1)),
        "ssm_b_re": nrm((N_SSM_LAYERS, SSM_GROUPS, SSM_STATE, SSM_GROUP), (2 * SSM_GROUP) ** -0.5),
        "ssm_b_im": nrm((N_SSM_LAYERS, SSM_GROUPS, SSM_STATE, SSM_GROUP), (2 * SSM_GROUP) ** -0.5),
        "ssm_c_re": nrm((N_SSM_LAYERS, SSM_GROUPS, SSM_GROUP, SSM_STATE), (2 * SSM_STATE) ** -0.5),
        "ssm_c_im": nrm((N_SSM_LAYERS, SSM_GROUPS, SSM_GROUP, SSM_STATE), (2 * SSM_STATE) ** -0.5),
        "ssm_d": nrm((N_SSM_LAYERS, d), 1.0),
        "ssm_w_glu": nrm((N_SSM_LAYERS, d, 2 * d), d ** -0.5),
        "ssm_b_glu": nrm((N_SSM_LAYERS, 2 * d), 0.02),
    }


def reference(x_prompt, x_sample, state_conv, cache_k, cache_v, state_ssm_re, state_ssm_im,
              page_table, c_prompt, c_sample, ln_gain, ada_w, ada_b, ffn_w13, ffn_w2,
              conv_w_in, conv_w, conv_w_out, attn_w_qkv, attn_q_gain, attn_k_gain,
              attn_logit_bias, attn_w_o, ssm_lambda_re, ssm_lambda_im, ssm_log_dt, ssm_b_re,
              ssm_b_im, ssm_c_re, ssm_c_im, ssm_d, ssm_w_glu, ssm_b_glu):
    p = dict(ln_gain=ln_gain, ada_w=ada_w, ada_b=ada_b, ffn_w13=ffn_w13, ffn_w2=ffn_w2,
             conv_w_in=conv_w_in, conv_w=conv_w, conv_w_out=conv_w_out,
             attn_w_qkv=attn_w_qkv, attn_q_gain=attn_q_gain, attn_k_gain=attn_k_gain,
             attn_w_o=attn_w_o, ssm_lambda_re=ssm_lambda_re, ssm_lambda_im=ssm_lambda_im,
             ssm_log_dt=ssm_log_dt, ssm_b_re=ssm_b_re, ssm_b_im=ssm_b_im, ssm_c_re=ssm_c_re,
             ssm_c_im=ssm_c_im, ssm_d=ssm_d, ssm_w_glu=ssm_w_glu, ssm_b_glu=ssm_b_glu)
    bp = x_prompt.shape[0]
    conv_zero = jnp.zeros((N_CONV_LAYERS, bp, CONV_WIDTH - 1, D_MODEL), x_prompt.dtype)
    ssm_zero = jnp.zeros((N_SSM_LAYERS, bp, SSM_GROUPS, SSM_STATE), jnp.float32)

    def attend_prompt(j, q, k, v):
        return sb_attention_prompt(q, k, v, attn_logit_bias[j])

    def attend_sample(j, q, k, v):
        return sb_attention_sample(q, k, v, cache_k[j], cache_v[j], page_table, attn_logit_bias[j])

    (y_prompt, prompt_conv_state, prompt_k, prompt_v,
     prompt_ssm_re, prompt_ssm_im) = trunk(x_prompt, c_prompt, conv_zero, ssm_zero, ssm_zero,
                                           attend_prompt, p)
    (y_sample, sample_conv_state, sample_k, sample_v,
     sample_ssm_re, sample_ssm_im) = trunk(x_sample, c_sample, state_conv, state_ssm_re,
                                           state_ssm_im, attend_sample, p)
    return (y_prompt, y_sample, prompt_conv_state, prompt_k, prompt_v, prompt_ssm_re, prompt_ssm_im,
            sample_conv_state, sample_k, sample_v, sample_ssm_re, sample_ssm_im)
```

```python
import functools

import jax
import jax.numpy as jnp
from jax import lax
from jax.experimental import pallas as pl
from jax.experimental.pallas import tpu as pltpu

F32 = jnp.float32
BF16 = jnp.bfloat16

N_MIXERS = 3
N_SUB = 3
RMS_EPS = 1e-6
FFN_RES_WEIGHT = 0.5
LAMBDA_RE_MAX = -1e-4
SUBLANES = 8
LANES = 128
MIB = 1024 * 1024


def _cparams(semantics, vmem_mib):
    return pltpu.CompilerParams(dimension_semantics=semantics,
                                vmem_limit_bytes=vmem_mib * MIB)


def _dot(a, b):
    return jnp.dot(a, b, preferred_element_type=F32)


def _dot_nt(a, b):
    return lax.dot_general(a, b, (((1,), (1,)), ((), ())), preferred_element_type=F32)


def _hi_lo(x):
    hi = x.astype(BF16)
    lo = (x - hi.astype(F32)).astype(BF16)
    return hi, lo


def _modnorm(x, gain, shift, scale):
    ms = jnp.mean(x * x, axis=-1, keepdims=True)
    y = x * lax.rsqrt(ms + RMS_EPS)
    return (y * gain) * (1.0 + scale) + shift


def _silu(x):
    return x * jax.nn.sigmoid(x)


def _mod_specs(mod5, layer, sub, tiles_per_seq):
    r, d = mod5.shape[3], mod5.shape[4]

    def spec(t):
        return pl.BlockSpec((None, None, None, r, d),
                            lambda i, *_: (layer, N_SUB * sub + t, i // tiles_per_seq, 0, 0))

    return spec(0), spec(1), spec(2)


def _ada_body(c_ref, w_ref, b_ref, op_ref, os_ref, *, rows_p):
    ca = _silu(c_ref[...]).astype(BF16)
    m = _dot(ca, w_ref[...].astype(BF16)) + b_ref[...]
    op_ref[...] = m[:rows_p]
    os_ref[...] = m[rows_p:]


def _ada_call(c_all, ada_w, ada_b, rows_p):
    depth, d, n = ada_w.shape
    nrow = n // d
    rows = c_all.shape[0]
    rows_s = rows - rows_p
    b4 = ada_b.reshape(depth, nrow, 1, d)
    return pl.pallas_call(
        functools.partial(_ada_body, rows_p=rows_p),
        grid=(depth, nrow),
        in_specs=[pl.BlockSpec((rows, d), lambda l, j: (0, 0)),
                  pl.BlockSpec((None, d, d), lambda l, j: (l, 0, j)),
                  pl.BlockSpec((None, None, 1, d), lambda l, j: (l, j, 0, 0))],
        out_specs=[pl.BlockSpec((None, None, rows_p, d), lambda l, j: (l, j, 0, 0)),
                   pl.BlockSpec((None, None, rows_s, d), lambda l, j: (l, j, 0, 0))],
        out_shape=[jax.ShapeDtypeStruct((depth, nrow, rows_p, d), F32),
                   jax.ShapeDtypeStruct((depth, nrow, rows_s, d), F32)],
        compiler_params=_cparams(("arbitrary", "arbitrary"), 32),
        name="ada_mod",
    )(c_all, ada_w, b4)


def _ffn_body(x_ref, g_ref, sh_ref, sc_ref, gt_ref, w1_ref, w3_ref, w2_ref, o_ref, h_scr, acc_scr):
    j = pl.program_id(1)

    @pl.when(j == 0)
    def _():
        h = _modnorm(x_ref[...], g_ref[...], sh_ref[...], sc_ref[...])
        h_scr[...] = h.astype(BF16)
        acc_scr[...] = jnp.zeros_like(acc_scr)

    h = h_scr[...]
    g = _dot(h, w1_ref[...].astype(BF16))
    u = _dot(h, w3_ref[...].astype(BF16))
    a = (_silu(g) * u).astype(BF16)
    acc_scr[...] += _dot(a, w2_ref[...].astype(BF16))

    @pl.when(j == pl.num_programs(1) - 1)
    def _():
        o_ref[...] = x_ref[...] + (FFN_RES_WEIGHT * gt_ref[...]) * acc_scr[...]


def _ffn_call(x, gain4, mod5, w13, w2, layer, which, sub, tm, tiles_per_seq, tf=256):
    t, d = x.shape
    dff = w2.shape[2]
    nf = dff // tf
    sh, sc, gt = _mod_specs(mod5, layer, sub, tiles_per_seq)
    return pl.pallas_call(
        _ffn_body,
        grid=(t // tm, nf),
        in_specs=[pl.BlockSpec((tm, d), lambda i, j: (i, 0)),
                  pl.BlockSpec((None, None, 1, d), lambda i, j: (layer, sub, 0, 0)),
                  sh, sc, gt,
                  pl.BlockSpec((None, None, d, tf), lambda i, j: (layer, which, 0, j)),
                  pl.BlockSpec((None, None, d, tf), lambda i, j: (layer, which, 0, j + nf)),
                  pl.BlockSpec((None, None, tf, d), lambda i, j: (layer, which, j, 0))],
        out_specs=pl.BlockSpec((tm, d), lambda i, j: (i, 0)),
        out_shape=jax.ShapeDtypeStruct((t, d), F32),
        scratch_shapes=[pltpu.VMEM((tm, d), BF16), pltpu.VMEM((tm, d), F32)],
        compiler_params=_cparams(("arbitrary", "arbitrary"), 48),
        name="ffn_swiglu",
    )(x, gain4, mod5, mod5, mod5, w13, w13, w2)


def _proj_res_body(x_ref, a_ref, gt_ref, w_ref, o_ref):
    o_ref[...] = x_ref[...] + gt_ref[...] * _dot(a_ref[...].astype(BF16), w_ref[...])


def _proj_res_call(x, a, mod5, w_bf, layer, tm, tiles_per_seq):
    t, d = x.shape
    _, _, gt = _mod_specs(mod5, layer, 1, tiles_per_seq)
    return pl.pallas_call(
        _proj_res_body,
        grid=(t // tm,),
        in_specs=[pl.BlockSpec((tm, d), lambda i: (i, 0)),
                  pl.BlockSpec((tm, a.shape[1]), lambda i: (i, 0)),
                  gt,
                  pl.BlockSpec(w_bf.shape, lambda i: (0, 0))],
        out_specs=pl.BlockSpec((tm, d), lambda i: (i, 0)),
        out_shape=jax.ShapeDtypeStruct((t, d), F32),
        compiler_params=_cparams(("arbitrary",), 32),
        name="proj_residual",
    )(x, a, mod5, w_bf)


def _conv_prompt_body(x_ref, g_ref, sh_ref, sc_ref, gt_ref, prev_ref, wb_ref, wc_ref, wx_ref, cw_ref,
                      wo_ref, o_ref, st_ref, h_scr, acc_scr, u_scr, *, tm, width):
    i = pl.program_id(1)
    j = pl.program_id(2)
    halo = SUBLANES

    @pl.when(j == 0)
    def _():
        h = _modnorm(x_ref[...], g_ref[...], sh_ref[...], sc_ref[...])
        h_scr[...] = h.astype(BF16)
        acc_scr[...] = jnp.zeros_like(acc_scr)

    @pl.when(i == 0)
    def _():
        u_scr[j, halo - (width - 1):halo, :] = prev_ref[...]

    h = h_scr[...]
    bg = _dot(h, wb_ref[...])
    cg = _dot(h, wc_ref[...])
    xi = _dot(h, wx_ref[...])
    u_scr[j, halo:halo + tm, :] = cg * xi
    cw = cw_ref[...]
    conv = u_scr[j, halo - (width - 1):halo - (width - 1) + tm, :] * cw[0:1]
    for tap in range(1, width):
        lo = halo - (width - 1) + tap
        conv = conv + u_scr[j, lo:lo + tm, :] * cw[tap:tap + 1]
    acc_scr[...] += _dot((bg * conv).astype(BF16), wo_ref[...])
    tail = u_scr[j, tm:tm + halo, :]
    u_scr[j, 0:halo, :] = tail
    st_ref[j] = tail[halo - (width - 1):]

    @pl.when(j == pl.num_programs(2) - 1)
    def _():
        o_ref[...] = x_ref[...] + gt_ref[...] * acc_scr[...]


def _conv_prompt_call(x, gain4, mod5, prev, w_in_bf, conv_w, w_out_bf, layer, nseq, tm, tc=256):
    t, d = x.shape
    seq = t // nseq
    ni = seq // tm
    nc = d // tc
    width = conv_w.shape[0]
    sh, sc, gt = _mod_specs(mod5, layer, 1, ni)

    def tok(b, i, j):
        return (b * ni + i, 0)

    def mspec(s):
        return pl.BlockSpec(s.block_shape, lambda b, i, j, _f=s.index_map: _f(b * ni + i))

    x_out, st = pl.pallas_call(
        functools.partial(_conv_prompt_body, tm=tm, width=width),
        grid=(nseq, ni, nc),
        in_specs=[pl.BlockSpec((tm, d), tok),
                  pl.BlockSpec((None, None, 1, d), lambda b, i, j: (layer, 1, 0, 0)),
                  mspec(sh), mspec(sc), mspec(gt),
                  pl.BlockSpec((None, width - 1, tc), lambda b, i, j: (b, 0, j)),
                  pl.BlockSpec((d, tc), lambda b, i, j: (0, j)),
                  pl.BlockSpec((d, tc), lambda b, i, j: (0, nc + j)),
                  pl.BlockSpec((d, tc), lambda b, i, j: (0, 2 * nc + j)),
                  pl.BlockSpec((width, tc), lambda b, i, j: (0, j)),
                  pl.BlockSpec((tc, d), lambda b, i, j: (j, 0))],
        out_specs=[pl.BlockSpec((tm, d), tok),
                   pl.BlockSpec((None, nc, width - 1, tc), lambda b, i, j: (b, 0, 0, 0))],
        out_shape=[jax.ShapeDtypeStruct((t, d), F32),
                   jax.ShapeDtypeStruct((nseq, nc, width - 1, tc), F32)],
        scratch_shapes=[pltpu.VMEM((tm, d), BF16), pltpu.VMEM((tm, d), F32),
                        pltpu.VMEM((nc, tm + SUBLANES, tc), F32)],
        compiler_params=_cparams(("arbitrary", "arbitrary", "arbitrary"), 40),
        name="conv_mixer_prompt",
    )(x, gain4, mod5, mod5, mod5, prev, w_in_bf, w_in_bf, w_in_bf, conv_w, w_out_bf)
    return x_out, st.transpose(0, 2, 1, 3).reshape(nseq, width - 1, d)


def _conv_step_body(x_ref, g_ref, sh_ref, sc_ref, gt_ref, prev_ref, wb_ref, wc_ref, wx_ref, cw_ref,
                    wo_ref, o_ref, st_ref, *, width):
    d = x_ref.shape[1]
    x = x_ref[...]
    h = _modnorm(x, g_ref[...], sh_ref[...], sc_ref[...]).astype(BF16)
    bg = _dot(h, wb_ref[...])
    u = _dot(h, wc_ref[...]) * _dot(h, wx_ref[...])
    cw = cw_ref[...]
    taps = [prev_ref[:, k * d:(k + 1) * d] for k in range(width - 1)] + [u]
    conv = taps[0] * cw[0:1]
    for k in range(1, width):
        conv = conv + taps[k] * cw[k:k + 1]
    o_ref[...] = x + gt_ref[...] * _dot((bg * conv).astype(BF16), wo_ref[...])
    for k in range(width - 1):
        st_ref[:, k * d:(k + 1) * d] = taps[k + 1]


def _conv_step_call(x, gain4, mod5, prev2, w_in_bf, conv_w, w_out_bf, layer):
    t, d = x.shape
    width = conv_w.shape[0]
    sh, sc, gt = _mod_specs(mod5, layer, 1, 1)
    full = lambda i: (0, 0)
    return pl.pallas_call(
        functools.partial(_conv_step_body, width=width),
        grid=(1,),
        in_specs=[pl.BlockSpec((t, d), full),
                  pl.BlockSpec((None, None, 1, d), lambda i: (layer, 1, 0, 0)),
                  sh, sc, gt,
                  pl.BlockSpec(prev2.shape, full),
                  pl.BlockSpec((d, d), lambda i: (0, 0)),
                  pl.BlockSpec((d, d), lambda i: (0, 1)),
                  pl.BlockSpec((d, d), lambda i: (0, 2)),
                  pl.BlockSpec(conv_w.shape, full),
                  pl.BlockSpec((d, d), full)],
        out_specs=[pl.BlockSpec((t, d), full), pl.BlockSpec(prev2.shape, full)],
        out_shape=[jax.ShapeDtypeStruct((t, d), F32), jax.ShapeDtypeStruct(prev2.shape, F32)],
        compiler_params=_cparams(("arbitrary",), 40),
        name="conv_mixer_step",
    )(x, gain4, mod5, mod5, mod5, prev2, w_in_bf, w_in_bf, w_in_bf, conv_w, w_out_bf)


def _qkv_body(x_ref, g_ref, sh_ref, sc_ref, wq_ref, wk_ref, wv_ref, qg_ref, kg_ref, seg_ref,
              *rest, head_dim, q_scale, feature_major):
    out_refs, h_scr = rest[:-1], rest[-1]
    j = pl.program_id(1)

    @pl.when(j == 0)
    def _():
        h = _modnorm(x_ref[...], g_ref[...], sh_ref[...], sc_ref[...])
        h_scr[...] = h.astype(BF16)

    h = h_scr[...]
    seg = seg_ref[...]

    def head_norm(y, gain):
        hi, lo = _hi_lo(y * y)
        ms = (_dot(hi, seg) + _dot(lo, seg)) * (1.0 / head_dim)
        return (y * lax.rsqrt(ms + RMS_EPS)) * gain

    q = head_norm(_dot(h, wq_ref[...]), qg_ref[...])
    k = head_norm(_dot(h, wk_ref[...]), kg_ref[...])
    v = _dot(h, wv_ref[...])
    if feature_major:
        qs_ref, k_ref, v_ref, t_scr = out_refs
        qs_ref[...] = (q * q_scale).T.astype(BF16)
        k_ref[...] = k.T
        t_scr[...] = v
        v_ref[...] = t_scr[...].T
    else:
        qs_ref, k_ref, kb_ref, v_ref, vb_ref = out_refs
        qs_ref[...] = (q * q_scale).astype(BF16)
        k_ref[...] = k
        kb_ref[...] = k.astype(BF16)
        v_ref[...] = v
        vb_ref[...] = v.astype(BF16)


def _qkv_call(x, gain4, mod5, w_qkv_bf, q_gain, k_gain, layer, tm, tiles_per_seq, head_dim,
              feature_major=False, tn=256):
    t, d = x.shape
    nn = d // tn
    sh, sc, _ = _mod_specs(mod5, layer, 1, tiles_per_seq)
    reps = tn // head_dim
    qg = jnp.tile(q_gain.astype(F32), reps).reshape(1, tn)
    kg = jnp.tile(k_gain.astype(F32), reps).reshape(1, tn)
    lane_head = jnp.arange(tn) // head_dim
    seg = (lane_head[:, None] == lane_head[None, :]).astype(BF16)
    const = lambda i, j: (0, 0)
    if feature_major:
        out_specs = [pl.BlockSpec((tn, tm), lambda i, j: (j, i))] * 3
        out_shape = [jax.ShapeDtypeStruct((d, t), BF16), jax.ShapeDtypeStruct((d, t), F32),
                     jax.ShapeDtypeStruct((d, t), F32)]
    else:
        out_specs = [pl.BlockSpec((tm, tn), lambda i, j: (i, j))] * 5
        out_shape = [jax.ShapeDtypeStruct((t, d), BF16), jax.ShapeDtypeStruct((t, d), F32),
                     jax.ShapeDtypeStruct((t, d), BF16), jax.ShapeDtypeStruct((t, d), F32),
                     jax.ShapeDtypeStruct((t, d), BF16)]
    return pl.pallas_call(
        functools.partial(_qkv_body, head_dim=head_dim, q_scale=head_dim ** -0.5,
                          feature_major=feature_major),
        grid=(t // tm, nn),
        in_specs=[pl.BlockSpec((tm, d), lambda i, j: (i, 0)),
                  pl.BlockSpec((None, None, 1, d), lambda i, j: (layer, 1, 0, 0)),
                  sh, sc,
                  pl.BlockSpec((d, tn), lambda i, j: (0, j)),
                  pl.BlockSpec((d, tn), lambda i, j: (0, nn + j)),
                  pl.BlockSpec((d, tn), lambda i, j: (0, 2 * nn + j)),
                  pl.BlockSpec((1, tn), const), pl.BlockSpec((1, tn), const),
                  pl.BlockSpec((tn, tn), const)],
        out_specs=out_specs,
        out_shape=out_shape,
        scratch_shapes=([pltpu.VMEM((tm, tn), F32)] if feature_major else []) + [pltpu.VMEM((tm, d), BF16)],
        compiler_params=_cparams(("arbitrary", "arbitrary"), 32),
        name="qkv_proj",
    )(x, gain4, mod5, mod5, w_qkv_bf, w_qkv_bf, w_qkv_bf, qg, kg, seg)


def _log_sigmoid_pair(z):
    sp = jnp.log(1.0 + jnp.exp(jnp.minimum(z, -z)))
    lb = jnp.minimum(z, 0.0) - sp
    return lb, lb - z


def _sb_prompt_body(bias_ref, q_ref, k_ref, v_ref, u_ref, o_ref, o_scr, c_scr, *, tq, tk, head_dim):
    hp = pl.program_id(1)
    qi = pl.program_id(2)
    n_diag = tq // tk
    q = q_ref[...]
    lane = lax.broadcasted_iota(jnp.int32, q.shape, 1)
    zero = jnp.zeros_like(q)
    q_heads = (jnp.where(lane < head_dim, q, zero), jnp.where(lane >= head_dim, q, zero))
    bias = (bias_ref[2 * hp], bias_ref[2 * hp + 1])
    u = u_ref[...]
    o_scr[...] = jnp.zeros_like(o_scr)
    c_scr[...] = jnp.zeros_like(c_scr)

    def chunk(kb, diag):
        off = pl.multiple_of(kb * tk, tk)
        kc = k_ref[pl.ds(off, tk), :]
        vc = v_ref[pl.ds(off, tk), :]
        if diag is not None:
            row = lax.broadcasted_iota(jnp.int32, (tq, tk), 0)
            col = lax.broadcasted_iota(jnp.int32, (tq, tk), 1)
            mask = (col + diag * tk) < row
        for hd in range(2):
            z = _dot_nt(q_heads[hd], kc) + bias[hd]
            lb, l1 = _log_sigmoid_pair(z)
            if diag is not None:
                l1 = jnp.where(mask, l1, 0.0)
            hi, lo = _hi_lo(l1)
            suffix = _dot(hi, u) + _dot(lo, u)
            c = c_scr[hd]
            w = jnp.exp(lb + (suffix + c))
            if diag is not None:
                w = jnp.where(mask, w, 0.0)
            c_scr[hd] = c + (suffix[:, 0:1] + l1[:, 0:1])
            o_scr[hd] += _dot(w.astype(BF16), vc)

    for dg in range(n_diag - 1, -1, -1):
        chunk(qi * n_diag + dg, dg)

    def body(jj, carry):
        chunk(qi * n_diag - 1 - jj, None)
        return carry

    lax.fori_loop(0, qi * n_diag, body, 0)
    o_ref[...] = jnp.where(lane < head_dim, o_scr[0], o_scr[1]).astype(BF16)


def _sb_prompt_call(qs, kb, vb, bias, nseq, head_dim, tq, tk):
    t, d = qs.shape
    seq = t // nseq
    nq = seq // tq
    pair = 2 * head_dim
    r = jnp.arange(tk)
    u = (r[:, None] > r[None, :]).astype(BF16)
    return pl.pallas_call(
        functools.partial(_sb_prompt_body, tq=tq, tk=tk, head_dim=head_dim),
        grid=(nseq, d // pair, nq),
        in_specs=[pl.BlockSpec(memory_space=pltpu.SMEM),
                  pl.BlockSpec((tq, pair), lambda b, p, i: (b * nq + i, p)),
                  pl.BlockSpec((seq, pair), lambda b, p, i: (b, p)),
                  pl.BlockSpec((seq, pair), lambda b, p, i: (b, p)),
                  pl.BlockSpec((tk, tk), lambda b, p, i: (0, 0))],
        out_specs=pl.BlockSpec((tq, pair), lambda b, p, i: (b * nq + i, p)),
        out_shape=jax.ShapeDtypeStruct((t, d), BF16),
        scratch_shapes=[pltpu.VMEM((2, tq, pair), F32), pltpu.VMEM((2, tq, 1), F32)],
        compiler_params=_cparams(("arbitrary", "arbitrary", "arbitrary"), 40),
        name="sb_attn_prompt",
    )(bias.astype(F32), qs, kb, vb, u)


def _sb_decode_body(pt_ref, q_ref, kn_ref, vn_ref, kc_ref, vc_ref, u_ref, bias_ref,
                    o_ref, q_scr, acc_scr, c_scr, *, n_past):
    seq = pl.program_id(0)
    j = pl.program_id(1)
    nh, hd, page = kc_ref.shape
    nseq = q_ref.shape[1]
    bias = bias_ref[...]

    def bf(x):
        return x.astype(BF16).astype(F32)

    def column(ref):
        lane = lax.broadcasted_iota(jnp.int32, ref.shape, 1)
        col = jnp.sum(jnp.where(lane == seq, bf(ref[...]), 0.0), axis=1, keepdims=True)
        return jnp.broadcast_to(col, (nh * hd, page)).reshape(nh, hd, page)

    def logits(keys_of_head):
        rows = [jnp.sum(bf(keys_of_head(h)) * q_scr[h], axis=0, keepdims=True) for h in range(nh)]
        return jnp.concatenate(rows, axis=0) + bias

    @pl.when(j == 0)
    def _():
        q_scr[...] = column(q_ref)
        kn = column(kn_ref)
        vn = column(vn_ref)
        q_pos = n_past
        k_pos = n_past
        lb, _ = _log_sigmoid_pair(logits(lambda h: kn[h]))
        w_new = bf(jnp.where(k_pos < q_pos, jnp.exp(lb), 0.0)) * (1.0 / page)
        for h in range(nh):
            acc_scr[h] = w_new[h:h + 1] * vn[h]
        c_scr[...] = jnp.zeros_like(c_scr)

    z = logits(lambda h: kc_ref[h])
    lb, l1 = _log_sigmoid_pair(z)
    hi, lo = _hi_lo(l1)
    u = u_ref[...]
    suffix = _dot(hi, u) + _dot(lo, u)
    c = c_scr[...]
    w = bf(jnp.exp(lb + (suffix + c)))
    c_scr[...] = c + (suffix[:, 0:1] + l1[:, 0:1])
    for h in range(nh):
        acc_scr[h] += w[h:h + 1] * bf(vc_ref[h])

    @pl.when(j == pl.num_programs(1) - 1)
    def _():
        o_ref[...] = jnp.sum(acc_scr[...], axis=-1)


def _sb_decode_call(qs_t, k_new_t, v_new_t, cache_k, cache_v, page_table, bias):
    d, b = qs_t.shape
    n_pages = page_table.shape[1]
    _, n_heads, head_dim, page = cache_k.shape
    r = jnp.arange(page)
    u = (r[:, None] > r[None, :]).astype(BF16)
    bias_col = bias.astype(F32).reshape(n_heads, 1)

    def cache(i, j, pt):
        return (pt[i * n_pages + (n_pages - 1 - j)], 0, 0, 0)

    const = lambda i, j, pt: (0, 0)
    grid_spec = pltpu.PrefetchScalarGridSpec(
        num_scalar_prefetch=1,
        grid=(b, n_pages),
        in_specs=[pl.BlockSpec((d, b), const), pl.BlockSpec((d, b), const), pl.BlockSpec((d, b), const),
                  pl.BlockSpec((None, n_heads, head_dim, page), cache),
                  pl.BlockSpec((None, n_heads, head_dim, page), cache),
                  pl.BlockSpec((page, page), const), pl.BlockSpec((n_heads, 1), const)],
        out_specs=pl.BlockSpec((None, n_heads, head_dim), lambda i, j, pt: (i, 0, 0)),
        scratch_shapes=[pltpu.VMEM((n_heads, head_dim, page), F32),
                        pltpu.VMEM((n_heads, head_dim, page), F32),
                        pltpu.VMEM((n_heads, 1), F32)],
    )
    return pl.pallas_call(
        functools.partial(_sb_decode_body, n_past=n_pages * page),
        grid_spec=grid_spec,
        out_shape=jax.ShapeDtypeStruct((b, n_heads, head_dim), F32),
        compiler_params=_cparams(("arbitrary", "arbitrary"), 32),
        name="sb_attn_decode",
    )(page_table.reshape(-1), qs_t, k_new_t, v_new_t, cache_k, cache_v, u, bias_col)


def _s5_constants(lam_re, lam_im, log_dt, b_re, b_im, c_re, c_im):
    lam_re = jnp.minimum(lam_re.astype(F32), LAMBDA_RE_MAX)
    lam_im = lam_im.astype(F32)
    dt = jnp.exp(log_dt.astype(F32))[:, None]
    decay = jnp.exp(lam_re * dt)
    a_re = decay * jnp.cos(lam_im * dt)
    a_im = decay * jnp.sin(lam_im * dt)
    inv = 1.0 / (lam_re * lam_re + lam_im * lam_im)
    f_re = ((a_re - 1.0) * lam_re + a_im * lam_im) * inv
    f_im = (a_im * lam_re - (a_re - 1.0) * lam_im) * inv
    b_re, b_im = b_re.astype(F32), b_im.astype(F32)
    bb_re = f_re[..., None] * b_re - f_im[..., None] * b_im
    bb_im = f_re[..., None] * b_im + f_im[..., None] * b_re
    g, p, c = bb_re.shape
    gl = LANES // c
    nkb = g // gl
    eye = jnp.eye(gl, dtype=F32)

    def in_map(bb):
        m = bb.transpose(0, 2, 1).reshape(nkb, gl, c, p)
        return jnp.einsum("kgcp,gh->kgchp", m, eye).reshape(nkb, gl * c, gl * p).astype(BF16)

    def out_map(cm):
        m = cm.astype(F32).transpose(0, 2, 1).reshape(nkb, gl, p, c)
        return jnp.einsum("kgpc,gh->kgphc", m, eye).reshape(nkb, gl * p, gl * c).astype(BF16)

    ar, ai = a_re.reshape(-1), a_im.reshape(-1)

    def cmul(x, y):
        return (x[0] * y[0] - x[1] * y[1], x[0] * y[1] + x[1] * y[0])

    a1 = (ar, ai)
    pw = [a1]
    for _ in range(SUBLANES - 1):
        pw.append(cmul(pw[-1], a1))
    rows = jnp.arange(SUBLANES)[:, None]
    steps = []
    for k in (1, 2, 4):
        steps.append(jnp.stack([jnp.where(rows >= k, pw[k - 1][0][None, :], 0.0),
                                jnp.where(rows >= k, pw[k - 1][1][None, :], 0.0)]))
    a_steps = jnp.stack(steps)
    a_rows = jnp.stack([jnp.stack([q[0] for q in pw]), jnp.stack([q[1] for q in pw])])
    a_one = jnp.stack([ar, ai]).reshape(2, 1, -1)
    return in_map(bb_re), in_map(bb_im), out_map(c_re), out_map(c_im), a_steps, a_rows, a_one


def _glu_out(y, wg_ref, bg_ref):
    z = _dot(y.astype(BF16), wg_ref[...]) + bg_ref[...]
    d = z.shape[1] // 2
    return z[:, :d] * jax.nn.sigmoid(z[:, d:])


def _s5_prompt_body(x_ref, g_ref, sh_ref, sc_ref, gt_ref, bbr_ref, bbi_ref, cr_ref, ci_ref, ak_ref,
                    ap_ref, dk_ref, wg_ref, bg_ref, o_ref, fre_ref, fim_ref,
                    y_scr, sr_scr, si_scr, st_scr, *, tl):
    i = pl.program_id(1)
    nkb, kin, cw = bbr_ref.shape
    ng = tl // SUBLANES

    @pl.when(i == 0)
    def _():
        st_scr[...] = jnp.zeros_like(st_scr)

    x = x_ref[...]
    h = _modnorm(x, g_ref[...], sh_ref[...], sc_ref[...])
    hb = h.astype(BF16)
    for kb in range(nkb):
        cols = slice(kb * cw, (kb + 1) * cw)
        hk = hb[:, kb * kin:(kb + 1) * kin]
        re = _dot(hk, bbr_ref[kb]).reshape(ng, SUBLANES, cw)
        im = _dot(hk, bbi_ref[kb]).reshape(ng, SUBLANES, cw)
        for ki in range(3):
            akr = ak_ref[ki, 0, :, cols]
            aki = ak_ref[ki, 1, :, cols]
            pr = pltpu.roll(re, 1 << ki, 1)
            pi = pltpu.roll(im, 1 << ki, 1)
            re, im = re + akr * pr - aki * pi, im + akr * pi + aki * pr
        sr_scr[...] = re
        si_scr[...] = im
        apr = ap_ref[0, :, cols]
        api = ap_ref[1, :, cols]

        def body(g, carry):
            s_r, s_i = carry
            nr = sr_scr[g] + apr * s_r - api * s_i
            ni = si_scr[g] + apr * s_i + api * s_r
            sr_scr[g] = nr
            si_scr[g] = ni
            last = SUBLANES - 1
            return (jnp.broadcast_to(nr[last:last + 1], nr.shape),
                    jnp.broadcast_to(ni[last:last + 1], ni.shape))

        s_r, s_i = lax.fori_loop(0, ng, body, (st_scr[0, :, cols], st_scr[1, :, cols]))
        st_scr[0, :, cols] = s_r
        st_scr[1, :, cols] = s_i
        s_re = sr_scr[...].reshape(tl, cw).astype(BF16)
        s_im = si_scr[...].reshape(tl, cw).astype(BF16)
        y_scr[:, kb * kin:(kb + 1) * kin] = _dot(s_re, cr_ref[kb]) - _dot(s_im, ci_ref[kb])
    y = y_scr[...] + dk_ref[...] * h
    o_ref[...] = x + gt_ref[...] * _glu_out(y, wg_ref, bg_ref)
    fre_ref[...] = st_scr[0, 0:1, :]
    fim_ref[...] = st_scr[1, 0:1, :]


def _s5_prompt_call(x, gain4, mod5, consts, d_skip, w_glu_bf, b_glu, layer, nseq, tl):
    t, d = x.shape
    ni = (t // nseq) // tl
    bbr, bbi, cr, ci, a_steps, a_rows, _ = consts
    nkb, kin, cw = bbr.shape
    nch = nkb * cw
    sh, sc, gt = _mod_specs(mod5, layer, 1, ni)

    def mspec(s):
        return pl.BlockSpec(s.block_shape, lambda b, i, _f=s.index_map: _f(b * ni + i))

    def whole(a):
        nd = a.ndim
        return pl.BlockSpec(a.shape, lambda b, i: (0,) * nd)

    dk = d_skip.astype(F32).reshape(1, d)
    bg = b_glu.astype(F32).reshape(1, -1)
    tok = lambda b, i: (b * ni + i, 0)
    x_out, fre, fim = pl.pallas_call(
        functools.partial(_s5_prompt_body, tl=tl),
        grid=(nseq, ni),
        in_specs=[pl.BlockSpec((tl, d), tok),
                  pl.BlockSpec((None, None, 1, d), lambda b, i: (layer, 1, 0, 0)),
                  mspec(sh), mspec(sc), mspec(gt),
                  whole(bbr), whole(bbi), whole(cr), whole(ci), whole(a_steps), whole(a_rows),
                  whole(dk), whole(w_glu_bf), whole(bg)],
        out_specs=[pl.BlockSpec((tl, d), tok),
                   pl.BlockSpec((None, 1, nch), lambda b, i: (b, 0, 0)),
                   pl.BlockSpec((None, 1, nch), lambda b, i: (b, 0, 0))],
        out_shape=[jax.ShapeDtypeStruct((t, d), F32),
                   jax.ShapeDtypeStruct((nseq, 1, nch), F32),
                   jax.ShapeDtypeStruct((nseq, 1, nch), F32)],
        scratch_shapes=[pltpu.VMEM((tl, d), F32),
                        pltpu.VMEM((tl // SUBLANES, SUBLANES, cw), F32),
                        pltpu.VMEM((tl // SUBLANES, SUBLANES, cw), F32),
                        pltpu.VMEM((2, SUBLANES, nch), F32)],
        compiler_params=_cparams(("arbitrary", "arbitrary"), 48),
        name="s5_mixer_prompt",
    )(x, gain4, mod5, mod5, mod5, bbr, bbi, cr, ci, a_steps, a_rows, dk, w_glu_bf, bg)
    return x_out, fre.reshape(nseq, nch), fim.reshape(nseq, nch)


def _s5_step_body(x_ref, g_ref, sh_ref, sc_ref, gt_ref, pre_ref, pim_ref, bbr_ref, bbi_ref, cr_ref,
                  ci_ref, a_ref, dk_ref, wg_ref, bg_ref, o_ref, nre_ref, nim_ref, y_scr):
    nkb, kin, cw = bbr_ref.shape
    x = x_ref[...]
    h = _modnorm(x, g_ref[...], sh_ref[...], sc_ref[...])
    hb = h.astype(BF16)
    for kb in range(nkb):
        cols = slice(kb * cw, (kb + 1) * cw)
        hk = hb[:, kb * kin:(kb + 1) * kin]
        ar = a_ref[0, :, cols]
        ai = a_ref[1, :, cols]
        pr = pre_ref[:, cols]
        pi = pim_ref[:, cols]
        s_r = _dot(hk, bbr_ref[kb]) + (ar * pr - ai * pi)
        s_i = _dot(hk, bbi_ref[kb]) + (ar * pi + ai * pr)
        nre_ref[:, cols] = s_r
        nim_ref[:, cols] = s_i
        y_scr[:, kb * kin:(kb + 1) * kin] = (_dot(s_r.astype(BF16), cr_ref[kb])
                                             - _dot(s_i.astype(BF16), ci_ref[kb]))
    y = y_scr[...] + dk_ref[...] * h
    o_ref[...] = x + gt_ref[...] * _glu_out(y, wg_ref, bg_ref)


def _s5_step_call(x, gain4, mod5, prev_re, prev_im, consts, d_skip, w_glu_bf, b_glu, layer):
    t, d = x.shape
    bbr, bbi, cr, ci, _, _, a_one = consts
    sh, sc, gt = _mod_specs(mod5, layer, 1, 1)

    def whole(a):
        nd = a.ndim
        return pl.BlockSpec(a.shape, lambda i: (0,) * nd)

    dk = d_skip.astype(F32).reshape(1, d)
    bg = b_glu.astype(F32).reshape(1, -1)
    return pl.pallas_call(
        _s5_step_body,
        grid=(1,),
        in_specs=[whole(x), pl.BlockSpec((None, None, 1, d), lambda i: (layer, 1, 0, 0)),
                  sh, sc, gt, whole(prev_re), whole(prev_im),
                  whole(bbr), whole(bbi), whole(cr), whole(ci), whole(a_one),
                  whole(dk), whole(w_glu_bf), whole(bg)],
        out_specs=[whole(x), whole(prev_re), whole(prev_im)],
        out_shape=[jax.ShapeDtypeStruct((t, d), F32), jax.ShapeDtypeStruct(prev_re.shape, F32),
                   jax.ShapeDtypeStruct(prev_im.shape, F32)],
        scratch_shapes=[pltpu.VMEM((t, d), F32)],
        compiler_params=_cparams(("arbitrary",), 48),
        name="s5_mixer_step",
    )(x, gain4, mod5, mod5, mod5, prev_re, prev_im, bbr, bbi, cr, ci, a_one, dk, w_glu_bf, bg)


def _tile(n, want):
    t = min(n, want)
    while n % t:
        t //= 2
    return t


def kernel(x_prompt, x_sample, state_conv, cache_k, cache_v, state_ssm_re, state_ssm_im, page_table, c_prompt, c_sample, ln_gain, ada_w, ada_b, ffn_w13, ffn_w2, conv_w_in, conv_w, conv_w_out, attn_w_qkv, attn_q_gain, attn_k_gain, attn_logit_bias, attn_w_o, ssm_lambda_re, ssm_lambda_im, ssm_log_dt, ssm_b_re, ssm_b_im, ssm_c_re, ssm_c_im, ssm_d, ssm_w_glu, ssm_b_glu):
    bp, seq, d = x_prompt.shape
    bs, seq_s, _ = x_sample.shape
    assert seq_s == 1, "the sample trunk handles one new token per sequence"
    depth = ln_gain.shape[0]
    n_heads, head_dim = cache_k.shape[3], cache_k.shape[4]
    page = cache_k.shape[2]
    width = conv_w.shape[1]
    n_state = ssm_lambda_re.shape[1] * ssm_lambda_re.shape[2]

    rows_p = -(-bp // SUBLANES) * SUBLANES
    c_all = jnp.concatenate([c_prompt, jnp.zeros((rows_p - bp, d), F32), c_sample], axis=0)
    mod_p, mod_s = _ada_call(c_all, ada_w, ada_b, rows_p)
    mod_p = mod_p.reshape(depth, N_SUB * 3, rows_p, 1, d)
    mod_s = mod_s.reshape(depth, N_SUB * 3, 1, bs, d)
    gain4 = ln_gain.reshape(depth, N_SUB, 1, d)

    xp = x_prompt.reshape(bp * seq, d)
    xs = x_sample.reshape(bs, d)
    tm_p = _tile(seq, 512)
    tps = seq // tm_p

    outs = dict(pc=[], pk=[], pv=[], pr=[], pi=[], sc=[], sk=[], sv=[], sr=[], si=[])
    for i in range(depth):
        kind, j = i % N_MIXERS, i // N_MIXERS
        xp = _ffn_call(xp, gain4, mod_p, ffn_w13, ffn_w2, i, 0, 0, tm_p, tps)
        xs = _ffn_call(xs, gain4, mod_s, ffn_w13, ffn_w2, i, 0, 0, bs, 1)
        if kind == 0:
            w_in = conv_w_in[j].astype(BF16)
            w_out = conv_w_out[j].astype(BF16)
            prev_p = jnp.zeros((bp, width - 1, d), F32)
            xp, st = _conv_prompt_call(xp, gain4, mod_p, prev_p, w_in, conv_w[j], w_out, i, bp, tm_p)
            outs["pc"].append(st)
            xs, st = _conv_step_call(xs, gain4, mod_s, state_conv[j].reshape(bs, (width - 1) * d),
                                     w_in, conv_w[j], w_out, i)
            outs["sc"].append(st.reshape(bs, width - 1, d))
        elif kind == 1:
            w_qkv = attn_w_qkv[j].astype(BF16)
            w_o = attn_w_o[j].astype(BF16)
            qs, k, kb, v, vb = _qkv_call(xp, gain4, mod_p, w_qkv, attn_q_gain[j], attn_k_gain[j], i,
                                         tm_p, tps, head_dim)
            outs["pk"].append(k.reshape(bp, seq, n_heads, head_dim))
            outs["pv"].append(v.reshape(bp, seq, n_heads, head_dim))
            o = _sb_prompt_call(qs, kb, vb, attn_logit_bias[j], bp, head_dim, _tile(seq, 512),
                                _tile(seq, 256))
            xp = _proj_res_call(xp, o, mod_p, w_o, i, tm_p, tps)
            qs_t, k_t, v_t = _qkv_call(xs, gain4, mod_s, w_qkv, attn_q_gain[j], attn_k_gain[j], i,
                                       bs, 1, head_dim, feature_major=True)
            outs["sk"].append(k_t.reshape(n_heads, head_dim, bs, 1).transpose(2, 3, 0, 1))
            outs["sv"].append(v_t.reshape(n_heads, head_dim, bs, 1).transpose(2, 3, 0, 1))
            o = _sb_decode_call(qs_t, k_t, v_t, cache_k[j].transpose(0, 2, 3, 1),
                                cache_v[j].transpose(0, 2, 3, 1), page_table, attn_logit_bias[j])
            xs = _proj_res_call(xs, o.reshape(bs, d), mod_s, w_o, i, bs, 1)
        else:
            consts = _s5_constants(ssm_lambda_re[j], ssm_lambda_im[j], ssm_log_dt[j], ssm_b_re[j],
                                   ssm_b_im[j], ssm_c_re[j], ssm_c_im[j])
            w_glu = ssm_w_glu[j].astype(BF16)
            xp, fre, fim = _s5_prompt_call(xp, gain4, mod_p, consts, ssm_d[j], w_glu, ssm_b_glu[j], i,
                                           bp, _tile(seq, 256))
            outs["pr"].append(fre.reshape(bp, -1, ssm_lambda_re.shape[2]))
            outs["pi"].append(fim.reshape(bp, -1, ssm_lambda_re.shape[2]))
            xs, nre, nim = _s5_step_call(xs, gain4, mod_s, state_ssm_re[j].reshape(bs, n_state),
                                         state_ssm_im[j].reshape(bs, n_state), consts, ssm_d[j], w_glu,
                                         ssm_b_glu[j], i)
            outs["sr"].append(nre.reshape(state_ssm_re.shape[1:]))
            outs["si"].append(nim.reshape(state_ssm_im.shape[1:]))
        xp = _ffn_call(xp, gain4, mod_p, ffn_w13, ffn_w2, i, 1, 2, tm_p, tps)
        xs = _ffn_call(xs, gain4, mod_s, ffn_w13, ffn_w2, i, 1, 2, bs, 1)

    st = {k: jnp.stack(v) for k, v in outs.items()}
    return (xp.reshape(bp, seq, d), xs.reshape(bs, 1, d), st["pc"], st["pk"], st["pv"], st["pr"],
            st["pi"], st["sc"], st["sk"], st["sv"], st["sr"], st["si"])
```

```python
import functools

import jax
import jax.numpy as jnp
from jax import lax
from jax.experimental import pallas as pl
from jax.experimental.pallas import tpu as pltpu

F32 = jnp.float32
BF16 = jnp.bfloat16

N_MIXERS = 3
N_SUB = 3
RMS_EPS = 1e-6
FFN_RES_WEIGHT = 0.5
LAMBDA_RE_MAX = -1e-4
MASKED_LOG = -1e30
SUBLANES = 8
LANES = 128
MIB = 1024 * 1024


def _cparams(semantics, vmem_mib):
    return pltpu.CompilerParams(dimension_semantics=semantics,
                                vmem_limit_bytes=vmem_mib * MIB)


def _dot(a, b):
    return jnp.dot(a, b, preferred_element_type=F32)


def _dot_nt(a, b):
    return lax.dot_general(a, b, (((1,), (1,)), ((), ())), preferred_element_type=F32)


def _hi_lo(x):
    hi = x.astype(BF16)
    lo = (x - hi.astype(F32)).astype(BF16)
    return hi, lo


def _modnorm(x, gain, shift, scale):
    ms = jnp.mean(x * x, axis=-1, keepdims=True)
    y = x * lax.rsqrt(ms + RMS_EPS)
    return (y * gain) * (1.0 + scale) + shift


def _silu(x):
    return x * jax.nn.sigmoid(x)


def _mod_specs(mod5, layer, sub, tiles_per_seq):
    r, d = mod5.shape[3], mod5.shape[4]

    def spec(t):
        return pl.BlockSpec((None, None, None, r, d),
                            lambda i, *_: (layer, N_SUB * sub + t, i // tiles_per_seq, 0, 0))

    return spec(0), spec(1), spec(2)


def _ada_body(c_ref, w_ref, b_ref, op_ref, os_ref, *, rows_p):
    ca = _silu(c_ref[...]).astype(BF16)
    m = _dot(ca, w_ref[...].astype(BF16)) + b_ref[...]
    op_ref[...] = m[:rows_p]
    os_ref[...] = m[rows_p:]


def _ada_call(c_all, ada_w, ada_b, rows_p):
    depth, d, n = ada_w.shape
    nrow = n // d
    rows = c_all.shape[0]
    rows_s = rows - rows_p
    b4 = ada_b.reshape(depth, nrow, 1, d)
    return pl.pallas_call(
        functools.partial(_ada_body, rows_p=rows_p),
        grid=(depth, nrow),
        in_specs=[pl.BlockSpec((rows, d), lambda l, j: (0, 0)),
                  pl.BlockSpec((None, d, d), lambda l, j: (l, 0, j)),
                  pl.BlockSpec((None, None, 1, d), lambda l, j: (l, j, 0, 0))],
        out_specs=[pl.BlockSpec((None, None, rows_p, d), lambda l, j: (l, j, 0, 0)),
                   pl.BlockSpec((None, None, rows_s, d), lambda l, j: (l, j, 0, 0))],
        out_shape=[jax.ShapeDtypeStruct((depth, nrow, rows_p, d), F32),
                   jax.ShapeDtypeStruct((depth, nrow, rows_s, d), F32)],
        compiler_params=_cparams(("arbitrary", "arbitrary"), 32),
        name="ada_mod",
    )(c_all, ada_w, b4)


def _ffn_body(x_ref, g_ref, sh_ref, sc_ref, gt_ref, w13_ref, w2_ref, o_ref, *, tf):
    dff = w2_ref.shape[0]
    x = x_ref[...]
    h = _modnorm(x, g_ref[...], sh_ref[...], sc_ref[...]).astype(BF16)
    parts = []
    for c in range(0, dff, tf):
        g = _dot(h, w13_ref[:, c:c + tf])
        u = _dot(h, w13_ref[:, dff + c:dff + c + tf])
        parts.append((_silu(g) * u).astype(BF16))
    a = jnp.concatenate(parts, axis=1)
    o_ref[...] = x + (FFN_RES_WEIGHT * gt_ref[...]) * _dot(a, w2_ref[...])


def _ffn_call(x, gain4, mod5, w13_bf, w2_bf, layer, which, sub, tm, tiles_per_seq, tf=256):
    t, d = x.shape
    dff = w2_bf.shape[2]
    sh, sc, gt = _mod_specs(mod5, layer, sub, tiles_per_seq)
    once = pl.Buffered(1)
    return pl.pallas_call(
        functools.partial(_ffn_body, tf=tf),
        grid=(t // tm,),
        in_specs=[pl.BlockSpec((tm, d), lambda i: (i, 0)),
                  pl.BlockSpec((None, None, 1, d), lambda i: (layer, sub, 0, 0)),
                  sh, sc, gt,
                  pl.BlockSpec((None, None, d, 2 * dff), lambda i: (layer, which, 0, 0),
                               pipeline_mode=once),
                  pl.BlockSpec((None, None, dff, d), lambda i: (layer, which, 0, 0),
                               pipeline_mode=once)],
        out_specs=pl.BlockSpec((tm, d), lambda i: (i, 0)),
        out_shape=jax.ShapeDtypeStruct((t, d), F32),
        compiler_params=_cparams(("arbitrary",), 56),
        name="ffn_swiglu",
    )(x, gain4, mod5, mod5, mod5, w13_bf, w2_bf)


def _proj_res_body(x_ref, a_ref, gt_ref, w_ref, o_ref):
    o_ref[...] = x_ref[...] + gt_ref[...] * _dot(a_ref[...].astype(BF16), w_ref[...])


def _proj_res_call(x, a, mod5, w_bf, layer, tm, tiles_per_seq):
    t, d = x.shape
    _, _, gt = _mod_specs(mod5, layer, 1, tiles_per_seq)
    return pl.pallas_call(
        _proj_res_body,
        grid=(t // tm,),
        in_specs=[pl.BlockSpec((tm, d), lambda i: (i, 0)),
                  pl.BlockSpec((tm, a.shape[1]), lambda i: (i, 0)),
                  gt,
                  pl.BlockSpec(w_bf.shape, lambda i: (0, 0))],
        out_specs=pl.BlockSpec((tm, d), lambda i: (i, 0)),
        out_shape=jax.ShapeDtypeStruct((t, d), F32),
        compiler_params=_cparams(("arbitrary",), 32),
        name="proj_residual",
    )(x, a, mod5, w_bf)


def _conv_prompt_body(x_ref, g_ref, sh_ref, sc_ref, gt_ref, prev_ref, win_ref, cw_ref, wo_ref,
                      o_ref, st_ref, u_scr, *, tm, tc, width):
    i = pl.program_id(1)
    d = x_ref.shape[1]
    halo = SUBLANES
    first = halo - (width - 1)

    @pl.when(i == 0)
    def _():
        u_scr[first:halo, :] = prev_ref[...]

    x = x_ref[...]
    h = _modnorm(x, g_ref[...], sh_ref[...], sc_ref[...]).astype(BF16)
    parts = []
    for c in range(0, d, tc):
        cols = slice(c, c + tc)
        bg = _dot(h, win_ref[:, cols])
        cg = _dot(h, win_ref[:, d + c:d + c + tc])
        xi = _dot(h, win_ref[:, 2 * d + c:2 * d + c + tc])
        u_scr[halo:halo + tm, cols] = cg * xi
        conv = u_scr[first:first + tm, cols] * cw_ref[0:1, cols]
        for tap in range(1, width):
            conv = conv + u_scr[first + tap:first + tap + tm, cols] * cw_ref[tap:tap + 1, cols]
        parts.append((bg * conv).astype(BF16))
    o_ref[...] = x + gt_ref[...] * _dot(jnp.concatenate(parts, axis=1), wo_ref[...])
    tail = u_scr[tm:tm + halo, :]
    u_scr[0:halo, :] = tail
    st_ref[...] = tail[first:]


def _conv_prompt_call(x, gain4, mod5, prev, w_in_bf, conv_w, w_out_bf, layer, nseq, tm, tc=256):
    t, d = x.shape
    seq = t // nseq
    ni = seq // tm
    width = conv_w.shape[0]
    sh, sc, gt = _mod_specs(mod5, layer, 1, ni)
    once = pl.Buffered(1)
    tok = lambda b, i: (b * ni + i, 0)
    const = lambda b, i: (0, 0)

    def mspec(s):
        return pl.BlockSpec(s.block_shape, lambda b, i, _f=s.index_map: _f(b * ni + i))

    return pl.pallas_call(
        functools.partial(_conv_prompt_body, tm=tm, tc=tc, width=width),
        grid=(nseq, ni),
        in_specs=[pl.BlockSpec((tm, d), tok),
                  pl.BlockSpec((None, None, 1, d), lambda b, i: (layer, 1, 0, 0)),
                  mspec(sh), mspec(sc), mspec(gt),
                  pl.BlockSpec((None, width - 1, d), lambda b, i: (b, 0, 0)),
                  pl.BlockSpec(w_in_bf.shape, const, pipeline_mode=once),
                  pl.BlockSpec(conv_w.shape, const),
                  pl.BlockSpec(w_out_bf.shape, const, pipeline_mode=once)],
        out_specs=[pl.BlockSpec((tm, d), tok),
                   pl.BlockSpec((None, width - 1, d), lambda b, i: (b, 0, 0))],
        out_shape=[jax.ShapeDtypeStruct((t, d), F32),
                   jax.ShapeDtypeStruct((nseq, width - 1, d), F32)],
        scratch_shapes=[pltpu.VMEM((tm + SUBLANES, d), F32)],
        compiler_params=_cparams(("arbitrary", "arbitrary"), 40),
        name="conv_mixer_prompt",
    )(x, gain4, mod5, mod5, mod5, prev, w_in_bf, conv_w, w_out_bf)


def _conv_step_body(x_ref, g_ref, sh_ref, sc_ref, gt_ref, prev_ref, wb_ref, wc_ref, wx_ref, cw_ref,
                    wo_ref, o_ref, st_ref, *, width):
    d = x_ref.shape[1]
    x = x_ref[...]
    h = _modnorm(x, g_ref[...], sh_ref[...], sc_ref[...]).astype(BF16)
    bg = _dot(h, wb_ref[...])
    u = _dot(h, wc_ref[...]) * _dot(h, wx_ref[...])
    cw = cw_ref[...]
    taps = [prev_ref[:, k * d:(k + 1) * d] for k in range(width - 1)] + [u]
    conv = taps[0] * cw[0:1]
    for k in range(1, width):
        conv = conv + taps[k] * cw[k:k + 1]
    o_ref[...] = x + gt_ref[...] * _dot((bg * conv).astype(BF16), wo_ref[...])
    for k in range(width - 1):
        st_ref[:, k * d:(k + 1) * d] = taps[k + 1]


def _conv_step_call(x, gain4, mod5, prev2, w_in_bf, conv_w, w_out_bf, layer):
    t, d = x.shape
    width = conv_w.shape[0]
    sh, sc, gt = _mod_specs(mod5, layer, 1, 1)
    full = lambda i: (0, 0)
    return pl.pallas_call(
        functools.partial(_conv_step_body, width=width),
        grid=(1,),
        in_specs=[pl.BlockSpec((t, d), full),
                  pl.BlockSpec((None, None, 1, d), lambda i: (layer, 1, 0, 0)),
                  sh, sc, gt,
                  pl.BlockSpec(prev2.shape, full),
                  pl.BlockSpec((d, d), lambda i: (0, 0)),
                  pl.BlockSpec((d, d), lambda i: (0, 1)),
                  pl.BlockSpec((d, d), lambda i: (0, 2)),
                  pl.BlockSpec(conv_w.shape, full),
                  pl.BlockSpec((d, d), full)],
        out_specs=[pl.BlockSpec((t, d), full), pl.BlockSpec(prev2.shape, full)],
        out_shape=[jax.ShapeDtypeStruct((t, d), F32), jax.ShapeDtypeStruct(prev2.shape, F32)],
        compiler_params=_cparams(("arbitrary",), 40),
        name="conv_mixer_step",
    )(x, gain4, mod5, mod5, mod5, prev2, w_in_bf, w_in_bf, w_in_bf, conv_w, w_out_bf)


def _qkv_body(x_ref, g_ref, sh_ref, sc_ref, w_ref, qg_ref, kg_ref, seg_ref, *rest,
              head_dim, q_scale, tn, decode):
    d = x_ref.shape[1]
    h = _modnorm(x_ref[...], g_ref[...], sh_ref[...], sc_ref[...]).astype(BF16)
    seg = seg_ref[...]

    def head_norm(y, gain):
        hi, lo = _hi_lo(y * y)
        ms = (_dot(hi, seg) + _dot(lo, seg)) * (1.0 / head_dim)
        return (y * lax.rsqrt(ms + RMS_EPS)) * gain

    for c in range(0, d, tn):
        cols = slice(c, c + tn)
        q = head_norm(_dot(h, w_ref[:, cols]), qg_ref[...]) * q_scale
        k = head_norm(_dot(h, w_ref[:, d + c:d + c + tn]), kg_ref[...])
        v = _dot(h, w_ref[:, 2 * d + c:2 * d + c + tn])
        if decode:
            qs_ref, kt_ref, vt_ref, t_scr = rest
            qs_ref[cols, :] = q.T.astype(BF16)
        else:
            qs_ref, kb_ref, vb_ref, kt_ref, vt_ref, t_scr = rest
            qs_ref[:, cols] = q.astype(BF16)
            kb_ref[:, cols] = k.astype(BF16)
            vb_ref[:, cols] = v.astype(BF16)
        kt_ref[cols, :] = k.T
        t_scr[...] = v
        vt_ref[cols, :] = t_scr[...].T


def _qkv_call(x, gain4, mod5, w_qkv_bf, q_gain, k_gain, layer, nseq, tm, head_dim, decode, tn=256):
    t, d = x.shape
    seq = t // nseq
    ni = seq // tm
    sh, sc, _ = _mod_specs(mod5, layer, 1, ni)
    reps = tn // head_dim
    qg = jnp.tile(q_gain.astype(F32), reps).reshape(1, tn)
    kg = jnp.tile(k_gain.astype(F32), reps).reshape(1, tn)
    lane_head = jnp.arange(tn) // head_dim
    seg = (lane_head[:, None] == lane_head[None, :]).astype(BF16)
    const = lambda i: (0, 0)
    tok = pl.BlockSpec((tm, d), lambda i: (i, 0))
    feat = pl.BlockSpec((None, d, tm), lambda i: (i // ni, 0, i % ni))
    feat_shape = jax.ShapeDtypeStruct((nseq, d, seq), F32)
    if decode:
        out_specs = [pl.BlockSpec((d, tm), lambda i: (0, i)), feat, feat]
        out_shape = [jax.ShapeDtypeStruct((d, t), BF16), feat_shape, feat_shape]
    else:
        out_specs = [tok, tok, tok, feat, feat]
        out_shape = [jax.ShapeDtypeStruct((t, d), BF16)] * 3 + [feat_shape, feat_shape]
    return pl.pallas_call(
        functools.partial(_qkv_body, head_dim=head_dim, q_scale=head_dim ** -0.5, tn=tn, decode=decode),
        grid=(t // tm,),
        in_specs=[tok,
                  pl.BlockSpec((None, None, 1, d), lambda i: (layer, 1, 0, 0)),
                  sh, sc,
                  pl.BlockSpec(w_qkv_bf.shape, const, pipeline_mode=pl.Buffered(1)),
                  pl.BlockSpec((1, tn), const), pl.BlockSpec((1, tn), const),
                  pl.BlockSpec((tn, tn), const)],
        out_specs=out_specs,
        out_shape=out_shape,
        scratch_shapes=[pltpu.VMEM((tm, tn), F32)],
        compiler_params=_cparams(("arbitrary",), 40),
        name="qkv_proj",
    )(x, gain4, mod5, mod5, w_qkv_bf, qg, kg, seg)


def _log_sigmoid_pair(z):
    sp = jnp.log(1.0 + jnp.exp(jnp.minimum(z, -z)))
    lb = jnp.minimum(z, 0.0) - sp
    return lb, lb - z


def _sb_prompt_body(bias_ref, q_ref, k_ref, v_ref, u_ref, o_ref, q2_scr, o_scr, c_scr, lb_scr, hi_scr,
                    lo_scr, l0_scr, *, tq, tk, head_dim):
    hp = pl.program_id(1)
    qi = pl.program_id(2)
    n_diag = tq // tk
    n_chunks = (qi + 1) * n_diag
    q = q_ref[...]
    lane = lax.broadcasted_iota(jnp.int32, q.shape, 1)
    zero = jnp.zeros_like(q)
    q2_scr[0:tq] = jnp.where(lane < head_dim, q, zero)
    q2_scr[tq:2 * tq] = jnp.where(lane >= head_dim, q, zero)
    row2 = lax.broadcasted_iota(jnp.int32, (2 * tq, 1), 0)
    bias = jnp.where(row2 < tq, bias_ref[2 * hp], bias_ref[2 * hp + 1])
    u = u_ref[...]
    o_scr[...] = jnp.zeros_like(o_scr)
    c_scr[...] = jnp.zeros_like(c_scr)

    def chunk_rows(m):
        return pl.ds(pl.multiple_of((n_chunks - 1 - m) * tk, tk), tk)

    def stage1(m, slot, diag):
        z = _dot_nt(q2_scr[...], k_ref[chunk_rows(m), :]) + bias
        lb, l1 = _log_sigmoid_pair(z)
        if diag is not None:
            row = lax.broadcasted_iota(jnp.int32, (2 * tq, tk), 0)
            row = jnp.where(row >= tq, row - tq, row)
            col = lax.broadcasted_iota(jnp.int32, (2 * tq, tk), 1)
            mask = (col + diag * tk) < row
            l1 = jnp.where(mask, l1, 0.0)
            lb = jnp.where(mask, lb, MASKED_LOG)
        hi, lo = _hi_lo(l1)
        lb_scr[slot] = lb
        hi_scr[slot] = hi
        lo_scr[slot] = lo
        l0_scr[slot] = l1[:, 0:1]

    def stage2(m, slot):
        suffix = _dot(hi_scr[slot], u) + _dot(lo_scr[slot], u)
        c = c_scr[...]
        w = jnp.exp(lb_scr[slot] + (suffix + c))
        c_scr[...] = c + (suffix[:, 0:1] + l0_scr[slot])
        o_scr[...] += _dot(w.astype(BF16), v_ref[chunk_rows(m), :])

    for m in range(n_diag):
        stage1(m, m % 2, n_diag - 1 - m)
        if m:
            stage2(m - 1, (m - 1) % 2)

    def body(jj, carry):
        m = n_diag + jj
        slot = lax.rem(m, 2)
        stage1(m, slot, None)
        stage2(m - 1, 1 - slot)
        return carry

    lax.fori_loop(0, n_chunks - n_diag, body, 0)
    stage2(n_chunks - 1, lax.rem(n_chunks - 1, 2))
    o_ref[...] = jnp.where(lane < head_dim, o_scr[0:tq], o_scr[tq:2 * tq]).astype(BF16)


def _sb_prompt_call(qs, kb, vb, bias, nseq, head_dim, tq, tk):
    t, d = qs.shape
    seq = t // nseq
    nq = seq // tq
    pair = 2 * head_dim
    r = jnp.arange(tk)
    u = (r[:, None] > r[None, :]).astype(BF16)
    return pl.pallas_call(
        functools.partial(_sb_prompt_body, tq=tq, tk=tk, head_dim=head_dim),
        grid=(nseq, d // pair, nq),
        in_specs=[pl.BlockSpec(memory_space=pltpu.SMEM),
                  pl.BlockSpec((tq, pair), lambda b, p, i: (b * nq + i, p)),
                  pl.BlockSpec((seq, pair), lambda b, p, i: (b, p)),
                  pl.BlockSpec((seq, pair), lambda b, p, i: (b, p)),
                  pl.BlockSpec((tk, tk), lambda b, p, i: (0, 0))],
        out_specs=pl.BlockSpec((tq, pair), lambda b, p, i: (b * nq + i, p)),
        out_shape=jax.ShapeDtypeStruct((t, d), BF16),
        scratch_shapes=[pltpu.VMEM((2 * tq, pair), BF16), pltpu.VMEM((2 * tq, pair), F32),
                        pltpu.VMEM((2 * tq, 1), F32),
                        pltpu.VMEM((2, 2 * tq, tk), F32), pltpu.VMEM((2, 2 * tq, tk), BF16),
                        pltpu.VMEM((2, 2 * tq, tk), BF16), pltpu.VMEM((2, 2 * tq, 1), F32)],
        compiler_params=_cparams(("arbitrary", "arbitrary", "arbitrary"), 40),
        name="sb_attn_prompt",
    )(bias.astype(F32), qs, kb, vb, u)


def _sb_decode_body(pt_ref, q_ref, kn_ref, vn_ref, *rest, n_past, pps):
    kc_refs, vc_refs = rest[:pps], rest[pps:2 * pps]
    u2_ref, bias_ref, o_ref, q_scr, acc_scr, c_scr = rest[2 * pps:]
    seq = pl.program_id(0)
    j = pl.program_id(1)
    nh, hd, page = kc_refs[0].shape
    sub = SUBLANES
    bias = bias_ref[...]

    def bf(x):
        return x.astype(BF16).astype(F32)

    def column(ref):
        lane = lax.broadcasted_iota(jnp.int32, ref.shape, 1)
        col = jnp.sum(jnp.where(lane == seq, bf(ref[...]), 0.0), axis=1, keepdims=True)
        return jnp.broadcast_to(col, (nh * hd, page)).reshape(nh, hd, page)

    def logits(keys_of_head):
        rows = []
        for h in range(nh):
            part = (bf(keys_of_head(h)) * q_scr[h]).reshape(hd // sub, sub, page).sum(axis=0)
            for s in (4, 2, 1):
                part = part + pltpu.roll(part, s, 0)
            rows.append(part)
        return jnp.concatenate(rows, axis=0) + bias

    def accumulate(w, vals_of_head):
        for h in range(nh):
            wh = w[sub * h:sub * (h + 1)]
            vals = bf(vals_of_head(h)).reshape(hd // sub, sub, page)
            acc_scr[h] += (vals * wh[None]).reshape(hd, page)

    @pl.when(j == 0)
    def _():
        q_scr[...] = column(q_ref)
        kn = column(kn_ref)
        vn = column(vn_ref)
        q_pos = n_past
        k_pos = n_past
        lb, _ = _log_sigmoid_pair(logits(lambda h: kn[h]))
        w_new = bf(jnp.where(k_pos < q_pos, jnp.exp(lb), 0.0)) * (1.0 / page)
        acc_scr[...] = jnp.zeros_like(acc_scr)
        accumulate(w_new, lambda h: vn[h])
        c_scr[...] = jnp.zeros_like(c_scr)

    u2 = u2_ref[...]
    for p in range(pps):
        z = logits(lambda h: kc_refs[p][h])
        lb, l1 = _log_sigmoid_pair(z)
        hi, lo = _hi_lo(l1)
        suffix = _dot(jnp.concatenate([hi, lo], axis=1), u2)
        c = c_scr[...]
        w = bf(jnp.exp(lb + (suffix + c)))
        c_scr[...] = c + (suffix[:, 0:1] + l1[:, 0:1])
        accumulate(w, lambda h: vc_refs[p][h])

    @pl.when(j == pl.num_programs(1) - 1)
    def _():
        o_ref[...] = jnp.sum(acc_scr[...], axis=-1)


def _sb_decode_call(qs_t, k_new_t, v_new_t, cache_k, cache_v, page_table, bias, pages_per_step=8):
    d, b = qs_t.shape
    n_pages = page_table.shape[1]
    _, n_heads, head_dim, page = cache_k.shape
    pps = _tile(n_pages, pages_per_step)
    r = jnp.arange(page)
    u = (r[:, None] > r[None, :]).astype(BF16)
    u2 = jnp.concatenate([u, u], axis=0)
    bias_col = jnp.repeat(bias.astype(F32), SUBLANES).reshape(n_heads * SUBLANES, 1)

    def cache(p):
        return lambda i, j, pt: (pt[i * n_pages + (n_pages - 1 - (j * pps + p))], 0, 0, 0)

    const = lambda i, j, pt: (0, 0)
    page_specs = [pl.BlockSpec((None, n_heads, head_dim, page), cache(p)) for p in range(pps)]
    grid_spec = pltpu.PrefetchScalarGridSpec(
        num_scalar_prefetch=1,
        grid=(b, n_pages // pps),
        in_specs=[pl.BlockSpec((d, b), const), pl.BlockSpec((d, b), const), pl.BlockSpec((d, b), const)]
                 + page_specs + page_specs
                 + [pl.BlockSpec((2 * page, page), const), pl.BlockSpec((n_heads * SUBLANES, 1), const)],
        out_specs=pl.BlockSpec((None, n_heads, head_dim), lambda i, j, pt: (i, 0, 0)),
        scratch_shapes=[pltpu.VMEM((n_heads, head_dim, page), F32),
                        pltpu.VMEM((n_heads, head_dim, page), F32),
                        pltpu.VMEM((n_heads * SUBLANES, 1), F32)],
    )
    return pl.pallas_call(
        functools.partial(_sb_decode_body, n_past=n_pages * page, pps=pps),
        grid_spec=grid_spec,
        out_shape=jax.ShapeDtypeStruct((b, n_heads, head_dim), F32),
        compiler_params=_cparams(("arbitrary", "arbitrary"), 40),
        name="sb_attn_decode",
    )(page_table.reshape(-1), qs_t, k_new_t, v_new_t, *([cache_k] * pps), *([cache_v] * pps), u2, bias_col)


def _s5_constants(lam_re, lam_im, log_dt, b_re, b_im, c_re, c_im):
    lam_re = jnp.minimum(lam_re.astype(F32), LAMBDA_RE_MAX)
    lam_im = lam_im.astype(F32)
    dt = jnp.exp(log_dt.astype(F32))[:, None]
    decay = jnp.exp(lam_re * dt)
    a_re = decay * jnp.cos(lam_im * dt)
    a_im = decay * jnp.sin(lam_im * dt)
    inv = 1.0 / (lam_re * lam_re + lam_im * lam_im)
    f_re = ((a_re - 1.0) * lam_re + a_im * lam_im) * inv
    f_im = (a_im * lam_re - (a_re - 1.0) * lam_im) * inv
    b_re, b_im = b_re.astype(F32), b_im.astype(F32)
    bb_re = f_re[..., None] * b_re - f_im[..., None] * b_im
    bb_im = f_re[..., None] * b_im + f_im[..., None] * b_re
    g, p, c = bb_re.shape
    gl = LANES // c
    nkb = g // gl
    eye = jnp.eye(gl, dtype=F32)

    def in_map(bb):
        m = bb.transpose(0, 2, 1).reshape(nkb, gl, c, p)
        return jnp.einsum("kgcp,gh->kgchp", m, eye).reshape(nkb, gl * c, gl * p).astype(BF16)

    def out_map(cm):
        m = cm.astype(F32).transpose(0, 2, 1).reshape(nkb, gl, p, c)
        return jnp.einsum("kgpc,gh->kgphc", m, eye).reshape(nkb, gl * p, gl * c).astype(BF16)

    ar, ai = a_re.reshape(-1), a_im.reshape(-1)

    def cmul(x, y):
        return (x[0] * y[0] - x[1] * y[1], x[0] * y[1] + x[1] * y[0])

    a1 = (ar, ai)
    pw = [a1]
    for _ in range(SUBLANES - 1):
        pw.append(cmul(pw[-1], a1))
    rows = jnp.arange(SUBLANES)[:, None]
    steps = []
    for k in (1, 2, 4):
        steps.append(jnp.stack([jnp.where(rows >= k, pw[k - 1][0][None, :], 0.0),
                                jnp.where(rows >= k, pw[k - 1][1][None, :], 0.0)]))
    a_steps = jnp.stack(steps)
    a_rows = jnp.stack([jnp.stack([q[0] for q in pw]), jnp.stack([q[1] for q in pw])])
    a_one = jnp.stack([ar, ai]).reshape(2, 1, -1)
    return in_map(bb_re), in_map(bb_im), out_map(c_re), out_map(c_im), a_steps, a_rows, a_one


def _glu_out(y, wg_ref, bg_ref):
    z = _dot(y.astype(BF16), wg_ref[...]) + bg_ref[...]
    d = z.shape[1] // 2
    return z[:, :d] * jax.nn.sigmoid(z[:, d:])


def _s5_prompt_body(x_ref, g_ref, sh_ref, sc_ref, gt_ref, bbr_ref, bbi_ref, cr_ref, ci_ref, ak_ref,
                    ap_ref, dk_ref, wg_ref, bg_ref, o_ref, fre_ref, fim_ref,
                    y_scr, sr_scr, si_scr, st_scr, *, tl):
    i = pl.program_id(1)
    nkb, kin, cw = bbr_ref.shape
    ng = tl // SUBLANES

    @pl.when(i == 0)
    def _():
        st_scr[...] = jnp.zeros_like(st_scr)

    x = x_ref[...]
    h = _modnorm(x, g_ref[...], sh_ref[...], sc_ref[...])
    hb = h.astype(BF16)
    for kb in range(nkb):
        cols = slice(kb * cw, (kb + 1) * cw)
        hk = hb[:, kb * kin:(kb + 1) * kin]
        re = _dot(hk, bbr_ref[kb]).reshape(ng, SUBLANES, cw)
        im = _dot(hk, bbi_ref[kb]).reshape(ng, SUBLANES, cw)
        for ki in range(3):
            akr = ak_ref[ki, 0, :, cols]
            aki = ak_ref[ki, 1, :, cols]
            pr = pltpu.roll(re, 1 << ki, 1)
            pi = pltpu.roll(im, 1 << ki, 1)
            re, im = re + akr * pr - aki * pi, im + akr * pi + aki * pr
        sr_scr[...] = re
        si_scr[...] = im
        apr = ap_ref[0, :, cols]
        api = ap_ref[1, :, cols]

        def body(g, carry):
            s_r, s_i = carry
            nr = sr_scr[g] + apr * s_r - api * s_i
            ni = si_scr[g] + apr * s_i + api * s_r
            sr_scr[g] = nr
            si_scr[g] = ni
            last = SUBLANES - 1
            return (jnp.broadcast_to(nr[last:last + 1], nr.shape),
                    jnp.broadcast_to(ni[last:last + 1], ni.shape))

        s_r, s_i = lax.fori_loop(0, ng, body, (st_scr[0, :, cols], st_scr[1, :, cols]))
        st_scr[0, :, cols] = s_r
        st_scr[1, :, cols] = s_i
        s_re = sr_scr[...].reshape(tl, cw).astype(BF16)
        s_im = si_scr[...].reshape(tl, cw).astype(BF16)
        y_scr[:, kb * kin:(kb + 1) * kin] = _dot(s_re, cr_ref[kb]) - _dot(s_im, ci_ref[kb])
    y = y_scr[...] + dk_ref[...] * h
    o_ref[...] = x + gt_ref[...] * _glu_out(y, wg_ref, bg_ref)
    fre_ref[...] = st_scr[0, 0:1, :]
    fim_ref[...] = st_scr[1, 0:1, :]


def _s5_prompt_call(x, gain4, mod5, consts, d_skip, w_glu_bf, b_glu, layer, nseq, tl):
    t, d = x.shape
    ni = (t // nseq) // tl
    bbr, bbi, cr, ci, a_steps, a_rows, _ = consts
    nkb, kin, cw = bbr.shape
    nch = nkb * cw
    sh, sc, gt = _mod_specs(mod5, layer, 1, ni)

    def mspec(s):
        return pl.BlockSpec(s.block_shape, lambda b, i, _f=s.index_map: _f(b * ni + i))

    def whole(a):
        nd = a.ndim
        return pl.BlockSpec(a.shape, lambda b, i: (0,) * nd)

    dk = d_skip.astype(F32).reshape(1, d)
    bg = b_glu.astype(F32).reshape(1, -1)
    tok = lambda b, i: (b * ni + i, 0)
    x_out, fre, fim = pl.pallas_call(
        functools.partial(_s5_prompt_body, tl=tl),
        grid=(nseq, ni),
        in_specs=[pl.BlockSpec((tl, d), tok),
                  pl.BlockSpec((None, None, 1, d), lambda b, i: (layer, 1, 0, 0)),
                  mspec(sh), mspec(sc), mspec(gt),
                  whole(bbr), whole(bbi), whole(cr), whole(ci), whole(a_steps), whole(a_rows),
                  whole(dk), whole(w_glu_bf), whole(bg)],
        out_specs=[pl.BlockSpec((tl, d), tok),
                   pl.BlockSpec((None, 1, nch), lambda b, i: (b, 0, 0)),
                   pl.BlockSpec((None, 1, nch), lambda b, i: (b, 0, 0))],
        out_shape=[jax.ShapeDtypeStruct((t, d), F32),
                   jax.ShapeDtypeStruct((nseq, 1, nch), F32),
                   jax.ShapeDtypeStruct((nseq, 1, nch), F32)],
        scratch_shapes=[pltpu.VMEM((tl, d), F32),
                        pltpu.VMEM((tl // SUBLANES, SUBLANES, cw), F32),
                        pltpu.VMEM((tl // SUBLANES, SUBLANES, cw), F32),
                        pltpu.VMEM((2, SUBLANES, nch), F32)],
        compiler_params=_cparams(("arbitrary", "arbitrary"), 48),
        name="s5_mixer_prompt",
    )(x, gain4, mod5, mod5, mod5, bbr, bbi, cr, ci, a_steps, a_rows, dk, w_glu_bf, bg)
    return x_out, fre.reshape(nseq, nch), fim.reshape(nseq, nch)


def _s5_step_body(x_ref, g_ref, sh_ref, sc_ref, gt_ref, pre_ref, pim_ref, bbr_ref, bbi_ref, cr_ref,
                  ci_ref, a_ref, dk_ref, wg_ref, bg_ref, o_ref, nre_ref, nim_ref, y_scr):
    nkb, kin, cw = bbr_ref.shape
    x = x_ref[...]
    h = _modnorm(x, g_ref[...], sh_ref[...], sc_ref[...])
    hb = h.astype(BF16)
    for kb in range(nkb):
        cols = slice(kb * cw, (kb + 1) * cw)
        hk = hb[:, kb * kin:(kb + 1) * kin]
        ar = a_ref[0, :, cols]
        ai = a_ref[1, :, cols]
        pr = pre_ref[:, cols]
        pi = pim_ref[:, cols]
        s_r = _dot(hk, bbr_ref[kb]) + (ar * pr - ai * pi)
        s_i = _dot(hk, bbi_ref[kb]) + (ar * pi + ai * pr)
        nre_ref[:, cols] = s_r
        nim_ref[:, cols] = s_i
        y_scr[:, kb * kin:(kb + 1) * kin] = (_dot(s_r.astype(BF16), cr_ref[kb])
                                             - _dot(s_i.astype(BF16), ci_ref[kb]))
    y = y_scr[...] + dk_ref[...] * h
    o_ref[...] = x + gt_ref[...] * _glu_out(y, wg_ref, bg_ref)


def _s5_step_call(x, gain4, mod5, prev_re, prev_im, consts, d_skip, w_glu_bf, b_glu, layer):
    t, d = x.shape
    bbr, bbi, cr, ci, _, _, a_one = consts
    sh, sc, gt = _mod_specs(mod5, layer, 1, 1)

    def whole(a):
        nd = a.ndim
        return pl.BlockSpec(a.shape, lambda i: (0,) * nd)

    dk = d_skip.astype(F32).reshape(1, d)
    bg = b_glu.astype(F32).reshape(1, -1)
    return pl.pallas_call(
        _s5_step_body,
        grid=(1,),
        in_specs=[whole(x), pl.BlockSpec((None, None, 1, d), lambda i: (layer, 1, 0, 0)),
                  sh, sc, gt, whole(prev_re), whole(prev_im),
                  whole(bbr), whole(bbi), whole(cr), whole(ci), whole(a_one),
                  whole(dk), whole(w_glu_bf), whole(bg)],
        out_specs=[whole(x), whole(prev_re), whole(prev_im)],
        out_shape=[jax.ShapeDtypeStruct((t, d), F32), jax.ShapeDtypeStruct(prev_re.shape, F32),
                   jax.ShapeDtypeStruct(prev_im.shape, F32)],
        scratch_shapes=[pltpu.VMEM((t, d), F32)],
        compiler_params=_cparams(("arbitrary",), 48),
        name="s5_mixer_step",
    )(x, gain4, mod5, mod5, mod5, prev_re, prev_im, bbr, bbi, cr, ci, a_one, dk, w_glu_bf, bg)


def _tile(n, want):
    t = min(n, want)
    while n % t:
        t //= 2
    return t


def kernel(x_prompt, x_sample, state_conv, cache_k, cache_v, state_ssm_re, state_ssm_im, page_table, c_prompt, c_sample, ln_gain, ada_w, ada_b, ffn_w13, ffn_w2, conv_w_in, conv_w, conv_w_out, attn_w_qkv, attn_q_gain, attn_k_gain, attn_logit_bias, attn_w_o, ssm_lambda_re, ssm_lambda_im, ssm_log_dt, ssm_b_re, ssm_b_im, ssm_c_re, ssm_c_im, ssm_d, ssm_w_glu, ssm_b_glu):
    bp, seq, d = x_prompt.shape
    bs, seq_s, _ = x_sample.shape
    assert seq_s == 1, "the sample trunk handles one new token per sequence"
    depth = ln_gain.shape[0]
    n_heads, head_dim = cache_k.shape[3], cache_k.shape[4]
    page = cache_k.shape[2]
    width = conv_w.shape[1]
    n_state = ssm_lambda_re.shape[1] * ssm_lambda_re.shape[2]

    rows_p = -(-bp // SUBLANES) * SUBLANES
    c_all = jnp.concatenate([c_prompt, jnp.zeros((rows_p - bp, d), F32), c_sample], axis=0)
    mod_p, mod_s = _ada_call(c_all, ada_w, ada_b, rows_p)
    mod_p = mod_p.reshape(depth, N_SUB * 3, rows_p, 1, d)
    mod_s = mod_s.reshape(depth, N_SUB * 3, 1, bs, d)
    gain4 = ln_gain.reshape(depth, N_SUB, 1, d)

    xp = x_prompt.reshape(bp * seq, d)
    xs = x_sample.reshape(bs, d)
    tm_p = _tile(seq, 512)
    tps = seq // tm_p
    w13_bf = ffn_w13.astype(BF16)
    w2_bf = ffn_w2.astype(BF16)

    outs = dict(pc=[], pk=[], pv=[], pr=[], pi=[], sc=[], sk=[], sv=[], sr=[], si=[])
    for i in range(depth):
        kind, j = i % N_MIXERS, i // N_MIXERS
        xp = _ffn_call(xp, gain4, mod_p, w13_bf, w2_bf, i, 0, 0, tm_p, tps)
        xs = _ffn_call(xs, gain4, mod_s, w13_bf, w2_bf, i, 0, 0, bs, 1)
        if kind == 0:
            w_in = conv_w_in[j].astype(BF16)
            w_out = conv_w_out[j].astype(BF16)
            prev_p = jnp.zeros((bp, width - 1, d), F32)
            xp, st = _conv_prompt_call(xp, gain4, mod_p, prev_p, w_in, conv_w[j], w_out, i, bp, tm_p)
            outs["pc"].append(st)
            xs, st = _conv_step_call(xs, gain4, mod_s, state_conv[j].reshape(bs, (width - 1) * d),
                                     w_in, conv_w[j], w_out, i)
            outs["sc"].append(st.reshape(bs, width - 1, d))
        elif kind == 1:
            w_qkv = attn_w_qkv[j].astype(BF16)
            w_o = attn_w_o[j].astype(BF16)
            qs, kb, vb, k_t, v_t = _qkv_call(xp, gain4, mod_p, w_qkv, attn_q_gain[j], attn_k_gain[j], i,
                                             bp, tm_p, head_dim, decode=False)
            outs["pk"].append(k_t.reshape(bp, n_heads, head_dim, seq).transpose(0, 3, 1, 2))
            outs["pv"].append(v_t.reshape(bp, n_heads, head_dim, seq).transpose(0, 3, 1, 2))
            o = _sb_prompt_call(qs, kb, vb, attn_logit_bias[j], bp, head_dim, _tile(seq, 512),
                                _tile(seq, 256))
            xp = _proj_res_call(xp, o, mod_p, w_o, i, tm_p, tps)
            qs_t, k_t, v_t = _qkv_call(xs, gain4, mod_s, w_qkv, attn_q_gain[j], attn_k_gain[j], i,
                                       1, bs, head_dim, decode=True)
            k_t, v_t = k_t[0], v_t[0]
            outs["sk"].append(k_t.reshape(n_heads, head_dim, bs, 1).transpose(2, 3, 0, 1))
            outs["sv"].append(v_t.reshape(n_heads, head_dim, bs, 1).transpose(2, 3, 0, 1))
            o = _sb_decode_call(qs_t, k_t, v_t, cache_k[j].transpose(0, 2, 3, 1),
                                cache_v[j].transpose(0, 2, 3, 1), page_table, attn_logit_bias[j])
            xs = _proj_res_call(xs, o.reshape(bs, d), mod_s, w_o, i, bs, 1)
        else:
            consts = _s5_constants(ssm_lambda_re[j], ssm_lambda_im[j], ssm_log_dt[j], ssm_b_re[j],
                                   ssm_b_im[j], ssm_c_re[j], ssm_c_im[j])
            w_glu = ssm_w_glu[j].astype(BF16)
            xp, fre, fim = _s5_prompt_call(xp, gain4, mod_p, consts, ssm_d[j], w_glu, ssm_b_glu[j], i,
                                           bp, _tile(seq, 256))
            outs["pr"].append(fre.reshape(bp, -1, ssm_lambda_re.shape[2]))
            outs["pi"].append(fim.reshape(bp, -1, ssm_lambda_re.shape[2]))
            xs, nre, nim = _s5_step_call(xs, gain4, mod_s, state_ssm_re[j].reshape(bs, n_state),
                                         state_ssm_im[j].reshape(bs, n_state), consts, ssm_d[j], w_glu,
                                         ssm_b_glu[j], i)
            outs["sr"].append(nre.reshape(state_ssm_re.shape[1:]))
            outs["si"].append(nim.reshape(state_ssm_im.shape[1:]))
        xp = _ffn_call(xp, gain4, mod_p, w13_bf, w2_bf, i, 1, 2, tm_p, tps)
        xs = _ffn_call(xs, gain4, mod_s, w13_bf, w2_bf, i, 1, 2, bs, 1)

    st = {k: jnp.stack(v) for k, v in outs.items()}
    return (xp.reshape(bp, seq, d), xs.reshape(bs, 1, d), st["pc"], st["pk"], st["pv"], st["pr"],
            st["pi"], st["sc"], st["sk"], st["sv"], st["sr"], st["si"])
```

```python
import functools

import jax
import jax.numpy as jnp
from jax import lax
from jax.experimental import pallas as pl
from jax.experimental.pallas import tpu as pltpu

F32 = jnp.float32
BF16 = jnp.bfloat16

N_MIXERS = 3
N_SUB = 3
RMS_EPS = 1e-6
FFN_RES_WEIGHT = 0.5
LAMBDA_RE_MAX = -1e-4
MASKED_LOG = -1e30
SUBLANES = 8
LANES = 128
MIB = 1024 * 1024


def _cparams(semantics, vmem_mib):
    return pltpu.CompilerParams(dimension_semantics=semantics,
                                vmem_limit_bytes=vmem_mib * MIB)


def _dot(a, b):
    return jnp.dot(a, b, preferred_element_type=F32)


def _dot_nt(a, b):
    return lax.dot_general(a, b, (((1,), (1,)), ((), ())), preferred_element_type=F32)


def _hi_lo(x):
    hi = x.astype(BF16)
    lo = (x - hi.astype(F32)).astype(BF16)
    return hi, lo


def _modnorm(x, gain, shift, scale):
    ms = jnp.mean(x * x, axis=-1, keepdims=True)
    y = x * lax.rsqrt(ms + RMS_EPS)
    return (y * gain) * (1.0 + scale) + shift


def _silu(x):
    return x * jax.nn.sigmoid(x)


def _mod_specs(mod5, layer, sub, tiles_per_seq):
    r, d = mod5.shape[3], mod5.shape[4]

    def spec(t):
        return pl.BlockSpec((None, None, None, r, d),
                            lambda i, *_: (layer, N_SUB * sub + t, i // tiles_per_seq, 0, 0))

    return spec(0), spec(1), spec(2)


def _ada_body(c_ref, w_ref, b_ref, op_ref, os_ref, *, rows_p):
    ca = _silu(c_ref[...]).astype(BF16)
    m = _dot(ca, w_ref[...].astype(BF16)) + b_ref[...]
    op_ref[...] = m[:rows_p]
    os_ref[...] = m[rows_p:]


def _ada_call(c_all, ada_w, ada_b, rows_p):
    depth, d, n = ada_w.shape
    nrow = n // d
    rows = c_all.shape[0]
    rows_s = rows - rows_p
    b4 = ada_b.reshape(depth, nrow, 1, d)
    return pl.pallas_call(
        functools.partial(_ada_body, rows_p=rows_p),
        grid=(depth, nrow),
        in_specs=[pl.BlockSpec((rows, d), lambda l, j: (0, 0)),
                  pl.BlockSpec((None, d, d), lambda l, j: (l, 0, j)),
                  pl.BlockSpec((None, None, 1, d), lambda l, j: (l, j, 0, 0))],
        out_specs=[pl.BlockSpec((None, None, rows_p, d), lambda l, j: (l, j, 0, 0)),
                   pl.BlockSpec((None, None, rows_s, d), lambda l, j: (l, j, 0, 0))],
        out_shape=[jax.ShapeDtypeStruct((depth, nrow, rows_p, d), F32),
                   jax.ShapeDtypeStruct((depth, nrow, rows_s, d), F32)],
        compiler_params=_cparams(("arbitrary", "arbitrary"), 32),
        name="ada_mod",
    )(c_all, ada_w, b4)


def _ffn_body(x_ref, g_ref, sh_ref, sc_ref, gt_ref, w13_ref, w2_ref, o_ref, *, tf):
    dff = w2_ref.shape[0]
    x = x_ref[...]
    h = _modnorm(x, g_ref[...], sh_ref[...], sc_ref[...]).astype(BF16)
    parts = []
    for c in range(0, dff, tf):
        g = _dot(h, w13_ref[:, c:c + tf])
        u = _dot(h, w13_ref[:, dff + c:dff + c + tf])
        parts.append((_silu(g) * u).astype(BF16))
    a = jnp.concatenate(parts, axis=1)
    o_ref[...] = x + (FFN_RES_WEIGHT * gt_ref[...]) * _dot(a, w2_ref[...])


def _ffn_call(x, gain4, mod5, w13_bf, w2_bf, layer, which, sub, tm, tiles_per_seq, tf=256):
    t, d = x.shape
    dff = w2_bf.shape[2]
    sh, sc, gt = _mod_specs(mod5, layer, sub, tiles_per_seq)
    once = pl.Buffered(1)
    return pl.pallas_call(
        functools.partial(_ffn_body, tf=tf),
        grid=(t // tm,),
        in_specs=[pl.BlockSpec((tm, d), lambda i: (i, 0)),
                  pl.BlockSpec((None, None, 1, d), lambda i: (layer, sub, 0, 0)),
                  sh, sc, gt,
                  pl.BlockSpec((None, None, d, 2 * dff), lambda i: (layer, which, 0, 0),
                               pipeline_mode=once),
                  pl.BlockSpec((None, None, dff, d), lambda i: (layer, which, 0, 0),
                               pipeline_mode=once)],
        out_specs=pl.BlockSpec((tm, d), lambda i: (i, 0)),
        out_shape=jax.ShapeDtypeStruct((t, d), F32),
        compiler_params=_cparams(("arbitrary",), 56),
        name="ffn_swiglu",
    )(x, gain4, mod5, mod5, mod5, w13_bf, w2_bf)


def _proj_res_body(x_ref, a_ref, gt_ref, w_ref, o_ref):
    o_ref[...] = x_ref[...] + gt_ref[...] * _dot(a_ref[...].astype(BF16), w_ref[...])


def _proj_res_call(x, a, mod5, w_bf, layer, tm, tiles_per_seq):
    t, d = x.shape
    _, _, gt = _mod_specs(mod5, layer, 1, tiles_per_seq)
    return pl.pallas_call(
        _proj_res_body,
        grid=(t // tm,),
        in_specs=[pl.BlockSpec((tm, d), lambda i: (i, 0)),
                  pl.BlockSpec((tm, a.shape[1]), lambda i: (i, 0)),
                  gt,
                  pl.BlockSpec(w_bf.shape, lambda i: (0, 0))],
        out_specs=pl.BlockSpec((tm, d), lambda i: (i, 0)),
        out_shape=jax.ShapeDtypeStruct((t, d), F32),
        compiler_params=_cparams(("arbitrary",), 32),
        name="proj_residual",
    )(x, a, mod5, w_bf)


def _conv_prompt_body(x_ref, g_ref, sh_ref, sc_ref, gt_ref, prev_ref, win_ref, cw_ref, wo_ref,
                      o_ref, st_ref, u_scr, *, tm, tc, width):
    i = pl.program_id(1)
    d = x_ref.shape[1]
    halo = SUBLANES
    first = halo - (width - 1)

    @pl.when(i == 0)
    def _():
        u_scr[first:halo, :] = prev_ref[...]

    x = x_ref[...]
    h = _modnorm(x, g_ref[...], sh_ref[...], sc_ref[...]).astype(BF16)
    parts = []
    for c in range(0, d, tc):
        cols = slice(c, c + tc)
        bg = _dot(h, win_ref[:, cols])
        cg = _dot(h, win_ref[:, d + c:d + c + tc])
        xi = _dot(h, win_ref[:, 2 * d + c:2 * d + c + tc])
        u_scr[halo:halo + tm, cols] = cg * xi
        conv = u_scr[first:first + tm, cols] * cw_ref[0:1, cols]
        for tap in range(1, width):
            conv = conv + u_scr[first + tap:first + tap + tm, cols] * cw_ref[tap:tap + 1, cols]
        parts.append((bg * conv).astype(BF16))
    o_ref[...] = x + gt_ref[...] * _dot(jnp.concatenate(parts, axis=1), wo_ref[...])
    tail = u_scr[tm:tm + halo, :]
    u_scr[0:halo, :] = tail
    st_ref[...] = tail[first:]


def _conv_prompt_call(x, gain4, mod5, prev, w_in_bf, conv_w, w_out_bf, layer, nseq, tm, tc=256):
    t, d = x.shape
    seq = t // nseq
    ni = seq // tm
    width = conv_w.shape[0]
    sh, sc, gt = _mod_specs(mod5, layer, 1, ni)
    once = pl.Buffered(1)
    tok = lambda b, i: (b * ni + i, 0)
    const = lambda b, i: (0, 0)

    def mspec(s):
        return pl.BlockSpec(s.block_shape, lambda b, i, _f=s.index_map: _f(b * ni + i))

    return pl.pallas_call(
        functools.partial(_conv_prompt_body, tm=tm, tc=tc, width=width),
        grid=(nseq, ni),
        in_specs=[pl.BlockSpec((tm, d), tok),
                  pl.BlockSpec((None, None, 1, d), lambda b, i: (layer, 1, 0, 0)),
                  mspec(sh), mspec(sc), mspec(gt),
                  pl.BlockSpec((None, width - 1, d), lambda b, i: (b, 0, 0)),
                  pl.BlockSpec(w_in_bf.shape, const, pipeline_mode=once),
                  pl.BlockSpec(conv_w.shape, const),
                  pl.BlockSpec(w_out_bf.shape, const, pipeline_mode=once)],
        out_specs=[pl.BlockSpec((tm, d), tok),
                   pl.BlockSpec((None, width - 1, d), lambda b, i: (b, 0, 0))],
        out_shape=[jax.ShapeDtypeStruct((t, d), F32),
                   jax.ShapeDtypeStruct((nseq, width - 1, d), F32)],
        scratch_shapes=[pltpu.VMEM((tm + SUBLANES, d), F32)],
        compiler_params=_cparams(("arbitrary", "arbitrary"), 40),
        name="conv_mixer_prompt",
    )(x, gain4, mod5, mod5, mod5, prev, w_in_bf, conv_w, w_out_bf)


def _conv_step_body(x_ref, g_ref, sh_ref, sc_ref, gt_ref, prev_ref, wb_ref, wc_ref, wx_ref, cw_ref,
                    wo_ref, o_ref, st_ref, *, width):
    d = x_ref.shape[1]
    x = x_ref[...]
    h = _modnorm(x, g_ref[...], sh_ref[...], sc_ref[...]).astype(BF16)
    bg = _dot(h, wb_ref[...])
    u = _dot(h, wc_ref[...]) * _dot(h, wx_ref[...])
    cw = cw_ref[...]
    taps = [prev_ref[:, k * d:(k + 1) * d] for k in range(width - 1)] + [u]
    conv = taps[0] * cw[0:1]
    for k in range(1, width):
        conv = conv + taps[k] * cw[k:k + 1]
    o_ref[...] = x + gt_ref[...] * _dot((bg * conv).astype(BF16), wo_ref[...])
    for k in range(width - 1):
        st_ref[:, k * d:(k + 1) * d] = taps[k + 1]


def _conv_step_call(x, gain4, mod5, prev2, w_in_bf, conv_w, w_out_bf, layer):
    t, d = x.shape
    width = conv_w.shape[0]
    sh, sc, gt = _mod_specs(mod5, layer, 1, 1)
    full = lambda i: (0, 0)
    return pl.pallas_call(
        functools.partial(_conv_step_body, width=width),
        grid=(1,),
        in_specs=[pl.BlockSpec((t, d), full),
                  pl.BlockSpec((None, None, 1, d), lambda i: (layer, 1, 0, 0)),
                  sh, sc, gt,
                  pl.BlockSpec(prev2.shape, full),
                  pl.BlockSpec((d, d), lambda i: (0, 0)),
                  pl.BlockSpec((d, d), lambda i: (0, 1)),
                  pl.BlockSpec((d, d), lambda i: (0, 2)),
                  pl.BlockSpec(conv_w.shape, full),
                  pl.BlockSpec((d, d), full)],
        out_specs=[pl.BlockSpec((t, d), full), pl.BlockSpec(prev2.shape, full)],
        out_shape=[jax.ShapeDtypeStruct((t, d), F32), jax.ShapeDtypeStruct(prev2.shape, F32)],
        compiler_params=_cparams(("arbitrary",), 40),
        name="conv_mixer_step",
    )(x, gain4, mod5, mod5, mod5, prev2, w_in_bf, w_in_bf, w_in_bf, conv_w, w_out_bf)


def _qkv_body(x_ref, g_ref, sh_ref, sc_ref, w_ref, qg_ref, kg_ref, seg_ref, *rest,
              head_dim, q_scale, tn, decode):
    d = x_ref.shape[1]
    h = _modnorm(x_ref[...], g_ref[...], sh_ref[...], sc_ref[...]).astype(BF16)
    seg = seg_ref[...]

    def head_norm(y, gain):
        hi, lo = _hi_lo(y * y)
        ms = (_dot(hi, seg) + _dot(lo, seg)) * (1.0 / head_dim)
        return (y * lax.rsqrt(ms + RMS_EPS)) * gain

    for c in range(0, d, tn):
        cols = slice(c, c + tn)
        q = head_norm(_dot(h, w_ref[:, cols]), qg_ref[...]) * q_scale
        k = head_norm(_dot(h, w_ref[:, d + c:d + c + tn]), kg_ref[...])
        v = _dot(h, w_ref[:, 2 * d + c:2 * d + c + tn])
        if decode:
            qs_ref, kt_ref, vt_ref, t_scr = rest
            qs_ref[cols, :] = q.T.astype(BF16)
        else:
            qs_ref, kb_ref, vb_ref, kt_ref, vt_ref, t_scr = rest
            qs_ref[:, cols] = q.astype(BF16)
            kb_ref[:, cols] = k.astype(BF16)
            vb_ref[:, cols] = v.astype(BF16)
        kt_ref[cols, :] = k.T
        t_scr[...] = v
        vt_ref[cols, :] = t_scr[...].T


def _qkv_call(x, gain4, mod5, w_qkv_bf, q_gain, k_gain, layer, nseq, tm, head_dim, decode, tn=256):
    t, d = x.shape
    seq = t // nseq
    ni = seq // tm
    sh, sc, _ = _mod_specs(mod5, layer, 1, ni)
    reps = tn // head_dim
    qg = jnp.tile(q_gain.astype(F32), reps).reshape(1, tn)
    kg = jnp.tile(k_gain.astype(F32), reps).reshape(1, tn)
    lane_head = jnp.arange(tn) // head_dim
    seg = (lane_head[:, None] == lane_head[None, :]).astype(BF16)
    const = lambda i: (0, 0)
    tok = pl.BlockSpec((tm, d), lambda i: (i, 0))
    feat = pl.BlockSpec((None, d, tm), lambda i: (i // ni, 0, i % ni))
    feat_shape = jax.ShapeDtypeStruct((nseq, d, seq), F32)
    if decode:
        out_specs = [pl.BlockSpec((d, tm), lambda i: (0, i)), feat, feat]
        out_shape = [jax.ShapeDtypeStruct((d, t), BF16), feat_shape, feat_shape]
    else:
        out_specs = [tok, tok, tok, feat, feat]
        out_shape = [jax.ShapeDtypeStruct((t, d), BF16)] * 3 + [feat_shape, feat_shape]
    return pl.pallas_call(
        functools.partial(_qkv_body, head_dim=head_dim, q_scale=head_dim ** -0.5, tn=tn, decode=decode),
        grid=(t // tm,),
        in_specs=[tok,
                  pl.BlockSpec((None, None, 1, d), lambda i: (layer, 1, 0, 0)),
                  sh, sc,
                  pl.BlockSpec(w_qkv_bf.shape, const, pipeline_mode=pl.Buffered(1)),
                  pl.BlockSpec((1, tn), const), pl.BlockSpec((1, tn), const),
                  pl.BlockSpec((tn, tn), const)],
        out_specs=out_specs,
        out_shape=out_shape,
        scratch_shapes=[pltpu.VMEM((tm, tn), F32)],
        compiler_params=_cparams(("arbitrary",), 40),
        name="qkv_proj",
    )(x, gain4, mod5, mod5, w_qkv_bf, qg, kg, seg)


def _log_sigmoid_pair(z):
    sign_bit = jnp.uint32(0x80000000)
    neg_abs = lax.bitcast_convert_type(lax.bitcast_convert_type(z, jnp.uint32) | sign_bit, F32)
    sp = jnp.log(1.0 + jnp.exp(neg_abs))
    lb = jnp.minimum(z, 0.0) - sp
    return lb, lb - z


def _sb_prompt_body(bias_ref, q_ref, k_ref, v_ref, u_ref, o_ref, q2_scr, o_scr, c_scr, lb_scr, hi_scr,
                    lo_scr, l0_scr, *, tq, tk, head_dim):
    hp = pl.program_id(1)
    qi = pl.program_id(2)
    n_diag = tq // tk
    n_chunks = (qi + 1) * n_diag
    q = q_ref[...]
    lane = lax.broadcasted_iota(jnp.int32, q.shape, 1)
    zero = jnp.zeros_like(q)
    q2_scr[0:tq] = jnp.where(lane < head_dim, q, zero)
    q2_scr[tq:2 * tq] = jnp.where(lane >= head_dim, q, zero)
    row2 = lax.broadcasted_iota(jnp.int32, (2 * tq, 1), 0)
    bias = jnp.where(row2 < tq, bias_ref[2 * hp], bias_ref[2 * hp + 1])
    u = u_ref[...]
    o_scr[...] = jnp.zeros_like(o_scr)
    c_scr[...] = jnp.zeros_like(c_scr)

    def chunk_rows(m):
        return pl.ds(pl.multiple_of((n_chunks - 1 - m) * tk, tk), tk)

    def stage1(m, slot, diag):
        z = _dot_nt(q2_scr[...], k_ref[chunk_rows(m), :]) + bias
        lb, l1 = _log_sigmoid_pair(z)
        if diag is not None:
            row = lax.broadcasted_iota(jnp.int32, (2 * tq, tk), 0)
            row = jnp.where(row >= tq, row - tq, row)
            col = lax.broadcasted_iota(jnp.int32, (2 * tq, tk), 1)
            mask = (col + diag * tk) < row
            l1 = jnp.where(mask, l1, 0.0)
            lb = jnp.where(mask, lb, MASKED_LOG)
        hi, lo = _hi_lo(l1)
        lb_scr[slot] = lb
        hi_scr[slot] = hi
        lo_scr[slot] = lo
        l0_scr[slot] = l1[:, 0:1]

    def stage2(m, slot):
        suffix = _dot(hi_scr[slot], u) + _dot(lo_scr[slot], u)
        c = c_scr[...]
        w = jnp.exp(lb_scr[slot] + (suffix + c))
        c_scr[...] = c + (suffix[:, 0:1] + l0_scr[slot])
        o_scr[...] += _dot(w.astype(BF16), v_ref[chunk_rows(m), :])

    for m in range(n_diag):
        stage1(m, m % 2, n_diag - 1 - m)
        if m:
            stage2(m - 1, (m - 1) % 2)

    def run(first, trips, unroll):
        def body(jj, carry):
            for k in range(unroll):
                m = first + unroll * jj + k
                stage1(m, (n_diag + k) % 2, None)
                stage2(m - 1, (n_diag + k - 1) % 2)
            return carry

        lax.fori_loop(0, trips, body, 0)

    rest = n_chunks - n_diag
    run(n_diag, rest // 4, 4)
    run(n_diag + (rest // 4) * 4, (rest % 4) // 2, 2)
    stage2(n_chunks - 1, (n_diag - 1) % 2)
    o_ref[...] = jnp.where(lane < head_dim, o_scr[0:tq], o_scr[tq:2 * tq]).astype(BF16)


def _sb_prompt_call(qs, kb, vb, bias, nseq, head_dim, tq, tk):
    t, d = qs.shape
    seq = t // nseq
    nq = seq // tq
    pair = 2 * head_dim
    assert (tq // tk) % 2 == 0, "the chunk loop is unrolled by two"
    r = jnp.arange(tk)
    u = (r[:, None] > r[None, :]).astype(BF16)
    return pl.pallas_call(
        functools.partial(_sb_prompt_body, tq=tq, tk=tk, head_dim=head_dim),
        grid=(nseq, d // pair, nq),
        in_specs=[pl.BlockSpec(memory_space=pltpu.SMEM),
                  pl.BlockSpec((tq, pair), lambda b, p, i: (b * nq + i, p)),
                  pl.BlockSpec((seq, pair), lambda b, p, i: (b, p)),
                  pl.BlockSpec((seq, pair), lambda b, p, i: (b, p)),
                  pl.BlockSpec((tk, tk), lambda b, p, i: (0, 0))],
        out_specs=pl.BlockSpec((tq, pair), lambda b, p, i: (b * nq + i, p)),
        out_shape=jax.ShapeDtypeStruct((t, d), BF16),
        scratch_shapes=[pltpu.VMEM((2 * tq, pair), BF16), pltpu.VMEM((2 * tq, pair), F32),
                        pltpu.VMEM((2 * tq, 1), F32),
                        pltpu.VMEM((2, 2 * tq, tk), F32), pltpu.VMEM((2, 2 * tq, tk), BF16),
                        pltpu.VMEM((2, 2 * tq, tk), BF16), pltpu.VMEM((2, 2 * tq, 1), F32)],
        compiler_params=_cparams(("arbitrary", "arbitrary", "arbitrary"), 40),
        name="sb_attn_prompt",
    )(bias.astype(F32), qs, kb, vb, u)


def _sb_decode_body(pt_ref, q_ref, kn_ref, vn_ref, *rest, n_past, pps):
    kc_refs, vc_refs = rest[:pps], rest[pps:2 * pps]
    u2_ref, bias_ref, o_ref, q_scr, acc_scr, c_scr = rest[2 * pps:]
    seq = pl.program_id(0)
    j = pl.program_id(1)
    nh, hd, page = kc_refs[0].shape
    sub = SUBLANES
    bias = bias_ref[...]

    def bf(x):
        return x.astype(BF16).astype(F32)

    def column(ref):
        lane = lax.broadcasted_iota(jnp.int32, ref.shape, 1)
        col = jnp.sum(jnp.where(lane == seq, bf(ref[...]), 0.0), axis=1, keepdims=True)
        return jnp.broadcast_to(col, (nh * hd, page)).reshape(nh, hd, page)

    def logits(keys_of_head):
        rows = []
        for h in range(nh):
            part = (bf(keys_of_head(h)) * q_scr[h]).reshape(hd // sub, sub, page).sum(axis=0)
            for s in (4, 2, 1):
                part = part + pltpu.roll(part, s, 0)
            rows.append(part)
        return jnp.concatenate(rows, axis=0) + bias

    def accumulate(w, vals_of_head):
        for h in range(nh):
            wh = w[sub * h:sub * (h + 1)]
            vals = bf(vals_of_head(h)).reshape(hd // sub, sub, page)
            acc_scr[h] += (vals * wh[None]).reshape(hd, page)

    @pl.when(j == 0)
    def _():
        q_scr[...] = column(q_ref)
        kn = column(kn_ref)
        vn = column(vn_ref)
        q_pos = n_past
        k_pos = n_past
        lb, _ = _log_sigmoid_pair(logits(lambda h: kn[h]))
        w_new = bf(jnp.where(k_pos < q_pos, jnp.exp(lb), 0.0)) * (1.0 / page)
        acc_scr[...] = jnp.zeros_like(acc_scr)
        accumulate(w_new, lambda h: vn[h])
        c_scr[...] = jnp.zeros_like(c_scr)

    u2 = u2_ref[...]
    for p in range(pps):
        z = logits(lambda h: kc_refs[p][h])
        lb, l1 = _log_sigmoid_pair(z)
        hi, lo = _hi_lo(l1)
        suffix = _dot(jnp.concatenate([hi, lo], axis=1), u2)
        c = c_scr[...]
        w = bf(jnp.exp(lb + (suffix + c)))
        c_scr[...] = c + (suffix[:, 0:1] + l1[:, 0:1])
        accumulate(w, lambda h: vc_refs[p][h])

    @pl.when(j == pl.num_programs(1) - 1)
    def _():
        o_ref[...] = jnp.sum(acc_scr[...], axis=-1)


def _sb_decode_call(qs_t, k_new_t, v_new_t, cache_k, cache_v, page_table, bias, pages_per_step=16):
    d, b = qs_t.shape
    n_pages = page_table.shape[1]
    _, n_heads, head_dim, page = cache_k.shape
    pps = _tile(n_pages, pages_per_step)
    r = jnp.arange(page)
    u = (r[:, None] > r[None, :]).astype(BF16)
    u2 = jnp.concatenate([u, u], axis=0)
    bias_col = jnp.repeat(bias.astype(F32), SUBLANES).reshape(n_heads * SUBLANES, 1)

    def cache(p):
        return lambda i, j, pt: (pt[i * n_pages + (n_pages - 1 - (j * pps + p))], 0, 0, 0)

    const = lambda i, j, pt: (0, 0)
    page_specs = [pl.BlockSpec((None, n_heads, head_dim, page), cache(p)) for p in range(pps)]
    grid_spec = pltpu.PrefetchScalarGridSpec(
        num_scalar_prefetch=1,
        grid=(b, n_pages // pps),
        in_specs=[pl.BlockSpec((d, b), const), pl.BlockSpec((d, b), const), pl.BlockSpec((d, b), const)]
                 + page_specs + page_specs
                 + [pl.BlockSpec((2 * page, page), const), pl.BlockSpec((n_heads * SUBLANES, 1), const)],
        out_specs=pl.BlockSpec((None, n_heads, head_dim), lambda i, j, pt: (i, 0, 0)),
        scratch_shapes=[pltpu.VMEM((n_heads, head_dim, page), F32),
                        pltpu.VMEM((n_heads, head_dim, page), F32),
                        pltpu.VMEM((n_heads * SUBLANES, 1), F32)],
    )
    return pl.pallas_call(
        functools.partial(_sb_decode_body, n_past=n_pages * page, pps=pps),
        grid_spec=grid_spec,
        out_shape=jax.ShapeDtypeStruct((b, n_heads, head_dim), F32),
        compiler_params=_cparams(("arbitrary", "arbitrary"), 48),
        name="sb_attn_decode",
    )(page_table.reshape(-1), qs_t, k_new_t, v_new_t, *([cache_k] * pps), *([cache_v] * pps), u2, bias_col)


def _s5_constants(lam_re, lam_im, log_dt, b_re, b_im, c_re, c_im):
    lam_re = jnp.minimum(lam_re.astype(F32), LAMBDA_RE_MAX)
    lam_im = lam_im.astype(F32)
    dt = jnp.exp(log_dt.astype(F32))[:, None]
    decay = jnp.exp(lam_re * dt)
    a_re = decay * jnp.cos(lam_im * dt)
    a_im = decay * jnp.sin(lam_im * dt)
    inv = 1.0 / (lam_re * lam_re + lam_im * lam_im)
    f_re = ((a_re - 1.0) * lam_re + a_im * lam_im) * inv
    f_im = (a_im * lam_re - (a_re - 1.0) * lam_im) * inv
    b_re, b_im = b_re.astype(F32), b_im.astype(F32)
    bb_re = f_re[..., None] * b_re - f_im[..., None] * b_im
    bb_im = f_re[..., None] * b_im + f_im[..., None] * b_re
    g, p, c = bb_re.shape
    gl = LANES // c
    nkb = g // gl
    eye = jnp.eye(gl, dtype=F32)

    def in_map(bb):
        m = bb.transpose(0, 2, 1).reshape(nkb, gl, c, p)
        return jnp.einsum("kgcp,gh->kgchp", m, eye).reshape(nkb, gl * c, gl * p).astype(BF16)

    def out_map(cm):
        m = cm.astype(F32).transpose(0, 2, 1).reshape(nkb, gl, p, c)
        return jnp.einsum("kgpc,gh->kgphc", m, eye).reshape(nkb, gl * p, gl * c).astype(BF16)

    ar, ai = a_re.reshape(-1), a_im.reshape(-1)

    def cmul(x, y):
        return (x[0] * y[0] - x[1] * y[1], x[0] * y[1] + x[1] * y[0])

    a1 = (ar, ai)
    pw = [a1]
    for _ in range(SUBLANES - 1):
        pw.append(cmul(pw[-1], a1))
    rows = jnp.arange(SUBLANES)[:, None]
    steps = []
    for k in (1, 2, 4):
        steps.append(jnp.stack([jnp.where(rows >= k, pw[k - 1][0][None, :], 0.0),
                                jnp.where(rows >= k, pw[k - 1][1][None, :], 0.0)]))
    a_steps = jnp.stack(steps)
    a_rows = jnp.stack([jnp.stack([q[0] for q in pw]), jnp.stack([q[1] for q in pw])])
    a_one = jnp.stack([ar, ai]).reshape(2, 1, -1)
    return in_map(bb_re), in_map(bb_im), out_map(c_re), out_map(c_im), a_steps, a_rows, a_one


def _glu_out(y, wg_ref, bg_ref):
    z = _dot(y.astype(BF16), wg_ref[...]) + bg_ref[...]
    d = z.shape[1] // 2
    return z[:, :d] * jax.nn.sigmoid(z[:, d:])


def _s5_prompt_body(x_ref, g_ref, sh_ref, sc_ref, gt_ref, bbr_ref, bbi_ref, cr_ref, ci_ref, ak_ref,
                    ap_ref, dk_ref, wg_ref, bg_ref, o_ref, fre_ref, fim_ref,
                    y_scr, sr_scr, si_scr, st_scr, *, tl):
    i = pl.program_id(1)
    nkb, kin, cw = bbr_ref.shape
    ng = tl // SUBLANES

    @pl.when(i == 0)
    def _():
        st_scr[...] = jnp.zeros_like(st_scr)

    x = x_ref[...]
    h = _modnorm(x, g_ref[...], sh_ref[...], sc_ref[...])
    hb = h.astype(BF16)
    for kb in range(nkb):
        cols = slice(kb * cw, (kb + 1) * cw)
        hk = hb[:, kb * kin:(kb + 1) * kin]
        re = _dot(hk, bbr_ref[kb]).reshape(ng, SUBLANES, cw)
        im = _dot(hk, bbi_ref[kb]).reshape(ng, SUBLANES, cw)
        for ki in range(3):
            akr = ak_ref[ki, 0, :, cols]
            aki = ak_ref[ki, 1, :, cols]
            pr = pltpu.roll(re, 1 << ki, 1)
            pi = pltpu.roll(im, 1 << ki, 1)
            re, im = re + akr * pr - aki * pi, im + akr * pi + aki * pr
        sr_scr[...] = re
        si_scr[...] = im
        apr = ap_ref[0, :, cols]
        api = ap_ref[1, :, cols]

        s_r = st_scr[0, :, cols]
        s_i = st_scr[1, :, cols]
        last = SUBLANES - 1
        for g in range(ng):
            nr = sr_scr[g] + apr * s_r - api * s_i
            ni = si_scr[g] + apr * s_i + api * s_r
            sr_scr[g] = nr
            si_scr[g] = ni
            s_r = jnp.broadcast_to(nr[last:last + 1], nr.shape)
            s_i = jnp.broadcast_to(ni[last:last + 1], ni.shape)
        st_scr[0, :, cols] = s_r
        st_scr[1, :, cols] = s_i
        s_re = sr_scr[...].reshape(tl, cw).astype(BF16)
        s_im = si_scr[...].reshape(tl, cw).astype(BF16)
        y_scr[:, kb * kin:(kb + 1) * kin] = _dot(s_re, cr_ref[kb]) - _dot(s_im, ci_ref[kb])
    y = y_scr[...] + dk_ref[...] * h
    o_ref[...] = x + gt_ref[...] * _glu_out(y, wg_ref, bg_ref)
    fre_ref[...] = st_scr[0, 0:1, :]
    fim_ref[...] = st_scr[1, 0:1, :]


def _s5_prompt_call(x, gain4, mod5, consts, d_skip, w_glu_bf, b_glu, layer, nseq, tl):
    t, d = x.shape
    ni = (t // nseq) // tl
    bbr, bbi, cr, ci, a_steps, a_rows, _ = consts
    nkb, kin, cw = bbr.shape
    nch = nkb * cw
    sh, sc, gt = _mod_specs(mod5, layer, 1, ni)

    def mspec(s):
        return pl.BlockSpec(s.block_shape, lambda b, i, _f=s.index_map: _f(b * ni + i))

    def whole(a):
        nd = a.ndim
        return pl.BlockSpec(a.shape, lambda b, i: (0,) * nd)

    dk = d_skip.astype(F32).reshape(1, d)
    bg = b_glu.astype(F32).reshape(1, -1)
    tok = lambda b, i: (b * ni + i, 0)
    x_out, fre, fim = pl.pallas_call(
        functools.partial(_s5_prompt_body, tl=tl),
        grid=(nseq, ni),
        in_specs=[pl.BlockSpec((tl, d), tok),
                  pl.BlockSpec((None, None, 1, d), lambda b, i: (layer, 1, 0, 0)),
                  mspec(sh), mspec(sc), mspec(gt),
                  whole(bbr), whole(bbi), whole(cr), whole(ci), whole(a_steps), whole(a_rows),
                  whole(dk), whole(w_glu_bf), whole(bg)],
        out_specs=[pl.BlockSpec((tl, d), tok),
                   pl.BlockSpec((None, 1, nch), lambda b, i: (b, 0, 0)),
                   pl.BlockSpec((None, 1, nch), lambda b, i: (b, 0, 0))],
        out_shape=[jax.ShapeDtypeStruct((t, d), F32),
                   jax.ShapeDtypeStruct((nseq, 1, nch), F32),
                   jax.ShapeDtypeStruct((nseq, 1, nch), F32)],
        scratch_shapes=[pltpu.VMEM((tl, d), F32),
                        pltpu.VMEM((tl // SUBLANES, SUBLANES, cw), F32),
                        pltpu.VMEM((tl // SUBLANES, SUBLANES, cw), F32),
                        pltpu.VMEM((2, SUBLANES, nch), F32)],
        compiler_params=_cparams(("arbitrary", "arbitrary"), 48),
        name="s5_mixer_prompt",
    )(x, gain4, mod5, mod5, mod5, bbr, bbi, cr, ci, a_steps, a_rows, dk, w_glu_bf, bg)
    return x_out, fre.reshape(nseq, nch), fim.reshape(nseq, nch)


def _s5_step_body(x_ref, g_ref, sh_ref, sc_ref, gt_ref, pre_ref, pim_ref, bbr_ref, bbi_ref, cr_ref,
                  ci_ref, a_ref, dk_ref, wg_ref, bg_ref, o_ref, nre_ref, nim_ref, y_scr):
    nkb, kin, cw = bbr_ref.shape
    x = x_ref[...]
    h = _modnorm(x, g_ref[...], sh_ref[...], sc_ref[...])
    hb = h.astype(BF16)
    for kb in range(nkb):
        cols = slice(kb * cw, (kb + 1) * cw)
        hk = hb[:, kb * kin:(kb + 1) * kin]
        ar = a_ref[0, :, cols]
        ai = a_ref[1, :, cols]
        pr = pre_ref[:, cols]
        pi = pim_ref[:, cols]
        s_r = _dot(hk, bbr_ref[kb]) + (ar * pr - ai * pi)
        s_i = _dot(hk, bbi_ref[kb]) + (ar * pi + ai * pr)
        nre_ref[:, cols] = s_r
        nim_ref[:, cols] = s_i
        y_scr[:, kb * kin:(kb + 1) * kin] = (_dot(s_r.astype(BF16), cr_ref[kb])
                                             - _dot(s_i.astype(BF16), ci_ref[kb]))
    y = y_scr[...] + dk_ref[...] * h
    o_ref[...] = x + gt_ref[...] * _glu_out(y, wg_ref, bg_ref)


def _s5_step_call(x, gain4, mod5, prev_re, prev_im, consts, d_skip, w_glu_bf, b_glu, layer):
    t, d = x.shape
    bbr, bbi, cr, ci, _, _, a_one = consts
    sh, sc, gt = _mod_specs(mod5, layer, 1, 1)

    def whole(a):
        nd = a.ndim
        return pl.BlockSpec(a.shape, lambda i: (0,) * nd)

    dk = d_skip.astype(F32).reshape(1, d)
    bg = b_glu.astype(F32).reshape(1, -1)
    return pl.pallas_call(
        _s5_step_body,
        grid=(1,),
        in_specs=[whole(x), pl.BlockSpec((None, None, 1, d), lambda i: (layer, 1, 0, 0)),
                  sh, sc, gt, whole(prev_re), whole(prev_im),
                  whole(bbr), whole(bbi), whole(cr), whole(ci), whole(a_one),
                  whole(dk), whole(w_glu_bf), whole(bg)],
        out_specs=[whole(x), whole(prev_re), whole(prev_im)],
        out_shape=[jax.ShapeDtypeStruct((t, d), F32), jax.ShapeDtypeStruct(prev_re.shape, F32),
                   jax.ShapeDtypeStruct(prev_im.shape, F32)],
        scratch_shapes=[pltpu.VMEM((t, d), F32)],
        compiler_params=_cparams(("arbitrary",), 48),
        name="s5_mixer_step",
    )(x, gain4, mod5, mod5, mod5, prev_re, prev_im, bbr, bbi, cr, ci, a_one, dk, w_glu_bf, bg)


def _tile(n, want):
    t = min(n, want)
    while n % t:
        t //= 2
    return t


def kernel(x_prompt, x_sample, state_conv, cache_k, cache_v, state_ssm_re, state_ssm_im, page_table, c_prompt, c_sample, ln_gain, ada_w, ada_b, ffn_w13, ffn_w2, conv_w_in, conv_w, conv_w_out, attn_w_qkv, attn_q_gain, attn_k_gain, attn_logit_bias, attn_w_o, ssm_lambda_re, ssm_lambda_im, ssm_log_dt, ssm_b_re, ssm_b_im, ssm_c_re, ssm_c_im, ssm_d, ssm_w_glu, ssm_b_glu):
    bp, seq, d = x_prompt.shape
    bs, seq_s, _ = x_sample.shape
    assert seq_s == 1, "the sample trunk handles one new token per sequence"
    depth = ln_gain.shape[0]
    n_heads, head_dim = cache_k.shape[3], cache_k.shape[4]
    page = cache_k.shape[2]
    width = conv_w.shape[1]
    n_state = ssm_lambda_re.shape[1] * ssm_lambda_re.shape[2]

    rows_p = -(-bp // SUBLANES) * SUBLANES
    c_all = jnp.concatenate([c_prompt, jnp.zeros((rows_p - bp, d), F32), c_sample], axis=0)
    mod_p, mod_s = _ada_call(c_all, ada_w, ada_b, rows_p)
    mod_p = mod_p.reshape(depth, N_SUB * 3, rows_p, 1, d)
    mod_s = mod_s.reshape(depth, N_SUB * 3, 1, bs, d)
    gain4 = ln_gain.reshape(depth, N_SUB, 1, d)

    xp = x_prompt.reshape(bp * seq, d)
    xs = x_sample.reshape(bs, d)
    tm_p = _tile(seq, 512)
    tps = seq // tm_p
    w13_bf = ffn_w13.astype(BF16)
    w2_bf = ffn_w2.astype(BF16)

    outs = dict(pc=[], pk=[], pv=[], pr=[], pi=[], sc=[], sk=[], sv=[], sr=[], si=[])
    for i in range(depth):
        kind, j = i % N_MIXERS, i // N_MIXERS
        xp = _ffn_call(xp, gain4, mod_p, w13_bf, w2_bf, i, 0, 0, tm_p, tps)
        xs = _ffn_call(xs, gain4, mod_s, w13_bf, w2_bf, i, 0, 0, bs, 1)
        if kind == 0:
            w_in = conv_w_in[j].astype(BF16)
            w_out = conv_w_out[j].astype(BF16)
            prev_p = jnp.zeros((bp, width - 1, d), F32)
            xp, st = _conv_prompt_call(xp, gain4, mod_p, prev_p, w_in, conv_w[j], w_out, i, bp, tm_p)
            outs["pc"].append(st)
            xs, st = _conv_step_call(xs, gain4, mod_s, state_conv[j].reshape(bs, (width - 1) * d),
                                     w_in, conv_w[j], w_out, i)
            outs["sc"].append(st.reshape(bs, width - 1, d))
        elif kind == 1:
            w_qkv = attn_w_qkv[j].astype(BF16)
            w_o = attn_w_o[j].astype(BF16)
            qs, kb, vb, k_t, v_t = _qkv_call(xp, gain4, mod_p, w_qkv, attn_q_gain[j], attn_k_gain[j], i,
                                             bp, tm_p, head_dim, decode=False)
            outs["pk"].append(k_t.reshape(bp, n_heads, head_dim, seq).transpose(0, 3, 1, 2))
            outs["pv"].append(v_t.reshape(bp, n_heads, head_dim, seq).transpose(0, 3, 1, 2))
            o = _sb_prompt_call(qs, kb, vb, attn_logit_bias[j], bp, head_dim, _tile(seq, 512),
                                _tile(seq, 256))
            xp = _proj_res_call(xp, o, mod_p, w_o, i, tm_p, tps)
            qs_t, k_t, v_t = _qkv_call(xs, gain4, mod_s, w_qkv, attn_q_gain[j], attn_k_gain[j], i,
                                       1, bs, head_dim, decode=True)
            k_t, v_t = k_t[0], v_t[0]
            outs["sk"].append(k_t.reshape(n_heads, head_dim, bs, 1).transpose(2, 3, 0, 1))
            outs["sv"].append(v_t.reshape(n_heads, head_dim, bs, 1).transpose(2, 3, 0, 1))
            o = _sb_decode_call(qs_t, k_t, v_t, cache_k[j].transpose(0, 2, 3, 1),
                                cache_v[j].transpose(0, 2, 3, 1), page_table, attn_logit_bias[j])
            xs = _proj_res_call(xs, o.reshape(bs, d), mod_s, w_o, i, bs, 1)
        else:
            consts = _s5_constants(ssm_lambda_re[j], ssm_lambda_im[j], ssm_log_dt[j], ssm_b_re[j],
                                   ssm_b_im[j], ssm_c_re[j], ssm_c_im[j])
            w_glu = ssm_w_glu[j].astype(BF16)
            xp, fre, fim = _s5_prompt_call(xp, gain4, mod_p, consts, ssm_d[j], w_glu, ssm_b_glu[j], i,
                                           bp, _tile(seq, 256))
            outs["pr"].append(fre.reshape(bp, -1, ssm_lambda_re.shape[2]))
            outs["pi"].append(fim.reshape(bp, -1, ssm_lambda_re.shape[2]))
            xs, nre, nim = _s5_step_call(xs, gain4, mod_s, state_ssm_re[j].reshape(bs, n_state),
                                         state_ssm_im[j].reshape(bs, n_state), consts, ssm_d[j], w_glu,
                                         ssm_b_glu[j], i)
            outs["sr"].append(nre.reshape(state_ssm_re.shape[1:]))
            outs["si"].append(nim.reshape(state_ssm_im.shape[1:]))
        xp = _ffn_call(xp, gain4, mod_p, w13_bf, w2_bf, i, 1, 2, tm_p, tps)
        xs = _ffn_call(xs, gain4, mod_s, w13_bf, w2_bf, i, 1, 2, bs, 1)

    st = {k: jnp.stack(v) for k, v in outs.items()}
    return (xp.reshape(bp, seq, d), xs.reshape(bs, 1, d), st["pc"], st["pk"], st["pv"], st["pr"],
            st["pi"], st["sc"], st["sk"], st["sv"], st["sr"], st["si"])
```

```python
import functools

import jax
import jax.numpy as jnp
from jax import lax
from jax.experimental import pallas as pl
from jax.experimental.pallas import tpu as pltpu

F32 = jnp.float32
BF16 = jnp.bfloat16

N_MIXERS = 3
N_SUB = 3
RMS_EPS = 1e-6
FFN_RES_WEIGHT = 0.5
LAMBDA_RE_MAX = -1e-4
MASKED_LOG = -1e30
SUBLANES = 8
LANES = 128
MIB = 1024 * 1024


def _cparams(semantics, vmem_mib):
    return pltpu.CompilerParams(dimension_semantics=semantics,
                                vmem_limit_bytes=vmem_mib * MIB)


def _dot(a, b):
    return jnp.dot(a, b, preferred_element_type=F32)


def _dot_nt(a, b):
    return lax.dot_general(a, b, (((1,), (1,)), ((), ())), preferred_element_type=F32)


def _hi_lo(x):
    hi = x.astype(BF16)
    lo = (x - hi.astype(F32)).astype(BF16)
    return hi, lo


def _modnorm(x, gain, shift, scale):
    ms = jnp.mean(x * x, axis=-1, keepdims=True)
    y = x * lax.rsqrt(ms + RMS_EPS)
    return (y * gain) * (1.0 + scale) + shift


def _silu(x):
    return x * jax.nn.sigmoid(x)


def _mod_specs(mod5, layer, sub, tiles_per_seq):
    r, d = mod5.shape[3], mod5.shape[4]

    def spec(t):
        return pl.BlockSpec((None, None, None, r, d),
                            lambda i, *_: (layer, N_SUB * sub + t, i // tiles_per_seq, 0, 0))

    return spec(0), spec(1), spec(2)


def _ada_body(c_ref, w_ref, b_ref, op_ref, os_ref, *, rows_p):
    ca = _silu(c_ref[...]).astype(BF16)
    m = _dot(ca, w_ref[...].astype(BF16)) + b_ref[...]
    op_ref[...] = m[:rows_p]
    os_ref[...] = m[rows_p:]


def _ada_call(c_all, ada_w, ada_b, rows_p):
    depth, d, n = ada_w.shape
    nrow = n // d
    rows = c_all.shape[0]
    rows_s = rows - rows_p
    b4 = ada_b.reshape(depth, nrow, 1, d)
    return pl.pallas_call(
        functools.partial(_ada_body, rows_p=rows_p),
        grid=(depth, nrow),
        in_specs=[pl.BlockSpec((rows, d), lambda l, j: (0, 0)),
                  pl.BlockSpec((None, d, d), lambda l, j: (l, 0, j)),
                  pl.BlockSpec((None, None, 1, d), lambda l, j: (l, j, 0, 0))],
        out_specs=[pl.BlockSpec((None, None, rows_p, d), lambda l, j: (l, j, 0, 0)),
                   pl.BlockSpec((None, None, rows_s, d), lambda l, j: (l, j, 0, 0))],
        out_shape=[jax.ShapeDtypeStruct((depth, nrow, rows_p, d), F32),
                   jax.ShapeDtypeStruct((depth, nrow, rows_s, d), F32)],
        compiler_params=_cparams(("arbitrary", "arbitrary"), 32),
        name="ada_mod",
    )(c_all, ada_w, b4)


def _ffn_body(x_ref, g_ref, sh_ref, sc_ref, gt_ref, w13_ref, w2_ref, o_ref, *, tf):
    dff = w2_ref.shape[0]
    x = x_ref[...]
    h = _modnorm(x, g_ref[...], sh_ref[...], sc_ref[...]).astype(BF16)
    parts = []
    for c in range(0, dff, tf):
        g = _dot(h, w13_ref[:, c:c + tf])
        u = _dot(h, w13_ref[:, dff + c:dff + c + tf])
        parts.append((_silu(g) * u).astype(BF16))
    a = jnp.concatenate(parts, axis=1)
    o_ref[...] = x + (FFN_RES_WEIGHT * gt_ref[...]) * _dot(a, w2_ref[...])


def _ffn_call(x, gain4, mod5, w13_bf, w2_bf, layer, which, sub, tm, tiles_per_seq, tf=256):
    t, d = x.shape
    dff = w2_bf.shape[2]
    sh, sc, gt = _mod_specs(mod5, layer, sub, tiles_per_seq)
    once = pl.Buffered(1)
    return pl.pallas_call(
        functools.partial(_ffn_body, tf=tf),
        grid=(t // tm,),
        in_specs=[pl.BlockSpec((tm, d), lambda i: (i, 0)),
                  pl.BlockSpec((None, None, 1, d), lambda i: (layer, sub, 0, 0)),
                  sh, sc, gt,
                  pl.BlockSpec((None, None, d, 2 * dff), lambda i: (layer, which, 0, 0),
                               pipeline_mode=once),
                  pl.BlockSpec((None, None, dff, d), lambda i: (layer, which, 0, 0),
                               pipeline_mode=once)],
        out_specs=pl.BlockSpec((tm, d), lambda i: (i, 0)),
        out_shape=jax.ShapeDtypeStruct((t, d), F32),
        compiler_params=_cparams(("arbitrary",), 56),
        name="ffn_swiglu",
    )(x, gain4, mod5, mod5, mod5, w13_bf, w2_bf)


def _proj_res_body(x_ref, a_ref, gt_ref, w_ref, o_ref):
    o_ref[...] = x_ref[...] + gt_ref[...] * _dot(a_ref[...].astype(BF16), w_ref[...])


def _proj_res_call(x, a, mod5, w_bf, layer, tm, tiles_per_seq):
    t, d = x.shape
    _, _, gt = _mod_specs(mod5, layer, 1, tiles_per_seq)
    return pl.pallas_call(
        _proj_res_body,
        grid=(t // tm,),
        in_specs=[pl.BlockSpec((tm, d), lambda i: (i, 0)),
                  pl.BlockSpec((tm, a.shape[1]), lambda i: (i, 0)),
                  gt,
                  pl.BlockSpec(w_bf.shape, lambda i: (0, 0))],
        out_specs=pl.BlockSpec((tm, d), lambda i: (i, 0)),
        out_shape=jax.ShapeDtypeStruct((t, d), F32),
        compiler_params=_cparams(("arbitrary",), 32),
        name="proj_residual",
    )(x, a, mod5, w_bf)


def _conv_prompt_body(x_ref, g_ref, sh_ref, sc_ref, gt_ref, prev_ref, win_ref, cw_ref, wo_ref,
                      o_ref, st_ref, u_scr, *, tm, tc, width):
    i = pl.program_id(1)
    d = x_ref.shape[1]
    halo = SUBLANES
    first = halo - (width - 1)

    @pl.when(i == 0)
    def _():
        u_scr[first:halo, :] = prev_ref[...]

    x = x_ref[...]
    h = _modnorm(x, g_ref[...], sh_ref[...], sc_ref[...]).astype(BF16)
    parts = []
    for c in range(0, d, tc):
        cols = slice(c, c + tc)
        bg = _dot(h, win_ref[:, cols])
        cg = _dot(h, win_ref[:, d + c:d + c + tc])
        xi = _dot(h, win_ref[:, 2 * d + c:2 * d + c + tc])
        u_scr[halo:halo + tm, cols] = cg * xi
        conv = u_scr[first:first + tm, cols] * cw_ref[0:1, cols]
        for tap in range(1, width):
            conv = conv + u_scr[first + tap:first + tap + tm, cols] * cw_ref[tap:tap + 1, cols]
        parts.append((bg * conv).astype(BF16))
    o_ref[...] = x + gt_ref[...] * _dot(jnp.concatenate(parts, axis=1), wo_ref[...])
    tail = u_scr[tm:tm + halo, :]
    u_scr[0:halo, :] = tail
    st_ref[...] = tail[first:]


def _conv_prompt_call(x, gain4, mod5, prev, w_in_bf, conv_w, w_out_bf, layer, nseq, tm, tc=256):
    t, d = x.shape
    seq = t // nseq
    ni = seq // tm
    width = conv_w.shape[0]
    sh, sc, gt = _mod_specs(mod5, layer, 1, ni)
    once = pl.Buffered(1)
    tok = lambda b, i: (b * ni + i, 0)
    const = lambda b, i: (0, 0)

    def mspec(s):
        return pl.BlockSpec(s.block_shape, lambda b, i, _f=s.index_map: _f(b * ni + i))

    return pl.pallas_call(
        functools.partial(_conv_prompt_body, tm=tm, tc=tc, width=width),
        grid=(nseq, ni),
        in_specs=[pl.BlockSpec((tm, d), tok),
                  pl.BlockSpec((None, None, 1, d), lambda b, i: (layer, 1, 0, 0)),
                  mspec(sh), mspec(sc), mspec(gt),
                  pl.BlockSpec((None, width - 1, d), lambda b, i: (b, 0, 0)),
                  pl.BlockSpec(w_in_bf.shape, const, pipeline_mode=once),
                  pl.BlockSpec(conv_w.shape, const),
                  pl.BlockSpec(w_out_bf.shape, const, pipeline_mode=once)],
        out_specs=[pl.BlockSpec((tm, d), tok),
                   pl.BlockSpec((None, width - 1, d), lambda b, i: (b, 0, 0))],
        out_shape=[jax.ShapeDtypeStruct((t, d), F32),
                   jax.ShapeDtypeStruct((nseq, width - 1, d), F32)],
        scratch_shapes=[pltpu.VMEM((tm + SUBLANES, d), F32)],
        compiler_params=_cparams(("arbitrary", "arbitrary"), 40),
        name="conv_mixer_prompt",
    )(x, gain4, mod5, mod5, mod5, prev, w_in_bf, conv_w, w_out_bf)


def _conv_step_body(x_ref, g_ref, sh_ref, sc_ref, gt_ref, prev_ref, wb_ref, wc_ref, wx_ref, cw_ref,
                    wo_ref, o_ref, st_ref, *, width):
    d = x_ref.shape[1]
    x = x_ref[...]
    h = _modnorm(x, g_ref[...], sh_ref[...], sc_ref[...]).astype(BF16)
    bg = _dot(h, wb_ref[...])
    u = _dot(h, wc_ref[...]) * _dot(h, wx_ref[...])
    cw = cw_ref[...]
    taps = [prev_ref[:, k * d:(k + 1) * d] for k in range(width - 1)] + [u]
    conv = taps[0] * cw[0:1]
    for k in range(1, width):
        conv = conv + taps[k] * cw[k:k + 1]
    o_ref[...] = x + gt_ref[...] * _dot((bg * conv).astype(BF16), wo_ref[...])
    for k in range(width - 1):
        st_ref[:, k * d:(k + 1) * d] = taps[k + 1]


def _conv_step_call(x, gain4, mod5, prev2, w_in_bf, conv_w, w_out_bf, layer):
    t, d = x.shape
    width = conv_w.shape[0]
    sh, sc, gt = _mod_specs(mod5, layer, 1, 1)
    full = lambda i: (0, 0)
    return pl.pallas_call(
        functools.partial(_conv_step_body, width=width),
        grid=(1,),
        in_specs=[pl.BlockSpec((t, d), full),
                  pl.BlockSpec((None, None, 1, d), lambda i: (layer, 1, 0, 0)),
                  sh, sc, gt,
                  pl.BlockSpec(prev2.shape, full),
                  pl.BlockSpec((d, d), lambda i: (0, 0)),
                  pl.BlockSpec((d, d), lambda i: (0, 1)),
                  pl.BlockSpec((d, d), lambda i: (0, 2)),
                  pl.BlockSpec(conv_w.shape, full),
                  pl.BlockSpec((d, d), full)],
        out_specs=[pl.BlockSpec((t, d), full), pl.BlockSpec(prev2.shape, full)],
        out_shape=[jax.ShapeDtypeStruct((t, d), F32), jax.ShapeDtypeStruct(prev2.shape, F32)],
        compiler_params=_cparams(("arbitrary",), 40),
        name="conv_mixer_step",
    )(x, gain4, mod5, mod5, mod5, prev2, w_in_bf, w_in_bf, w_in_bf, conv_w, w_out_bf)


def _qkv_body(x_ref, g_ref, sh_ref, sc_ref, w_ref, qg_ref, kg_ref, seg_ref, *rest,
              head_dim, q_scale, tn, decode):
    d = x_ref.shape[1]
    h = _modnorm(x_ref[...], g_ref[...], sh_ref[...], sc_ref[...]).astype(BF16)
    seg = seg_ref[...]

    def head_norm(y, gain):
        hi, lo = _hi_lo(y * y)
        ms = (_dot(hi, seg) + _dot(lo, seg)) * (1.0 / head_dim)
        return (y * lax.rsqrt(ms + RMS_EPS)) * gain

    for c in range(0, d, tn):
        cols = slice(c, c + tn)
        q = head_norm(_dot(h, w_ref[:, cols]), qg_ref[...]) * q_scale
        k = head_norm(_dot(h, w_ref[:, d + c:d + c + tn]), kg_ref[...])
        v = _dot(h, w_ref[:, 2 * d + c:2 * d + c + tn])
        if decode:
            qs_ref, kt_ref, vt_ref, t_scr = rest
            qs_ref[cols, :] = q.T.astype(BF16)
        else:
            qs_ref, kb_ref, vb_ref, kt_ref, vt_ref, t_scr = rest
            qs_ref[:, cols] = q.astype(BF16)
            kb_ref[:, cols] = k.astype(BF16)
            vb_ref[:, cols] = v.astype(BF16)
        kt_ref[cols, :] = k.T
        t_scr[...] = v
        vt_ref[cols, :] = t_scr[...].T


def _qkv_call(x, gain4, mod5, w_qkv_bf, q_gain, k_gain, layer, nseq, tm, head_dim, decode, tn=256):
    t, d = x.shape
    seq = t // nseq
    ni = seq // tm
    sh, sc, _ = _mod_specs(mod5, layer, 1, ni)
    reps = tn // head_dim
    qg = jnp.tile(q_gain.astype(F32), reps).reshape(1, tn)
    kg = jnp.tile(k_gain.astype(F32), reps).reshape(1, tn)
    lane_head = jnp.arange(tn) // head_dim
    seg = (lane_head[:, None] == lane_head[None, :]).astype(BF16)
    const = lambda i: (0, 0)
    tok = pl.BlockSpec((tm, d), lambda i: (i, 0))
    feat = pl.BlockSpec((None, d, tm), lambda i: (i // ni, 0, i % ni))
    feat_shape = jax.ShapeDtypeStruct((nseq, d, seq), F32)
    if decode:
        out_specs = [pl.BlockSpec((d, tm), lambda i: (0, i)), feat, feat]
        out_shape = [jax.ShapeDtypeStruct((d, t), BF16), feat_shape, feat_shape]
    else:
        out_specs = [tok, tok, tok, feat, feat]
        out_shape = [jax.ShapeDtypeStruct((t, d), BF16)] * 3 + [feat_shape, feat_shape]
    return pl.pallas_call(
        functools.partial(_qkv_body, head_dim=head_dim, q_scale=head_dim ** -0.5, tn=tn, decode=decode),
        grid=(t // tm,),
        in_specs=[tok,
                  pl.BlockSpec((None, None, 1, d), lambda i: (layer, 1, 0, 0)),
                  sh, sc,
                  pl.BlockSpec(w_qkv_bf.shape, const, pipeline_mode=pl.Buffered(1)),
                  pl.BlockSpec((1, tn), const), pl.BlockSpec((1, tn), const),
                  pl.BlockSpec((tn, tn), const)],
        out_specs=out_specs,
        out_shape=out_shape,
        scratch_shapes=[pltpu.VMEM((tm, tn), F32)],
        compiler_params=_cparams(("arbitrary",), 40),
        name="qkv_proj",
    )(x, gain4, mod5, mod5, w_qkv_bf, qg, kg, seg)


def _log_sigmoid_pair(z):
    sign_bit = jnp.uint32(0x80000000)
    neg_abs = lax.bitcast_convert_type(lax.bitcast_convert_type(z, jnp.uint32) | sign_bit, F32)
    sp = jnp.log(1.0 + jnp.exp(neg_abs))
    lb = jnp.minimum(z, 0.0) - sp
    return lb, lb - z


def _sb_prompt_body(bias_ref, q_ref, k_ref, v_ref, u_ref, o_ref, q2_scr, o_scr, c_scr, lb_scr,
                    hi_scr, lo_scr, l0_scr, *, tq, tk, head_dim):
    hp = pl.program_id(1)
    qi = pl.program_id(2)
    n_diag = tq // tk
    n_chunks = (qi + 1) * n_diag
    q = q_ref[...]
    lane = lax.broadcasted_iota(jnp.int32, q.shape, 1)
    zero = jnp.zeros_like(q)
    q2_scr[0:tq] = jnp.where(lane < head_dim, q, zero)
    q2_scr[tq:2 * tq] = jnp.where(lane >= head_dim, q, zero)
    row2 = lax.broadcasted_iota(jnp.int32, (2 * tq, 1), 0)
    bias = jnp.where(row2 < tq, bias_ref[2 * hp], bias_ref[2 * hp + 1])
    u = u_ref[...]
    o_scr[...] = jnp.zeros_like(o_scr)
    c_scr[...] = jnp.zeros_like(c_scr)

    def chunk_rows(m):
        return pl.ds(pl.multiple_of((n_chunks - 1 - m) * tk, tk), tk)

    def stage1(m, slot, diag):
        z = _dot_nt(q2_scr[...], k_ref[chunk_rows(m), :]) + bias
        lb, l1 = _log_sigmoid_pair(z)
        if diag is not None:
            row = lax.broadcasted_iota(jnp.int32, (2 * tq, tk), 0)
            row = jnp.where(row >= tq, row - tq, row)
            col = lax.broadcasted_iota(jnp.int32, (2 * tq, tk), 1)
            mask = (col + diag * tk) < row
            l1 = jnp.where(mask, l1, 0.0)
            lb = jnp.where(mask, lb, MASKED_LOG)
        hi, lo = _hi_lo(l1)
        lb_scr[slot] = lb
        hi_scr[slot] = hi
        lo_scr[slot] = lo
        l0_scr[slot] = l1[:, 0:1]

    def stage2(m, slot):
        suffix = _dot(hi_scr[slot], u) + _dot(lo_scr[slot], u)
        c = c_scr[...]
        w = jnp.exp(lb_scr[slot] + (suffix + c))
        c_scr[...] = c + (suffix[:, 0:1] + l0_scr[slot])
        o_scr[...] += _dot(w.astype(BF16), v_ref[chunk_rows(m), :])

    for m in range(n_diag):
        stage1(m, m % 2, n_diag - 1 - m)
        if m:
            stage2(m - 1, (m - 1) % 2)

    def run(first, trips, unroll):
        def body(jj, carry):
            for k in range(unroll):
                m = first + unroll * jj + k
                stage1(m, (n_diag + k) % 2, None)
                stage2(m - 1, (n_diag + k - 1) % 2)
            return carry

        lax.fori_loop(0, trips, body, 0)

    rest = n_chunks - n_diag
    run(n_diag, rest // 4, 4)
    run(n_diag + (rest // 4) * 4, (rest % 4) // 2, 2)
    stage2(n_chunks - 1, (n_diag - 1) % 2)
    o_ref[...] = jnp.where(lane < head_dim, o_scr[0:tq], o_scr[tq:2 * tq]).astype(BF16)


def _sb_prompt_call(qs, kb, vb, bias, nseq, head_dim, tq, tk):
    t, d = qs.shape
    seq = t // nseq
    nq = seq // tq
    pair = 2 * head_dim
    assert (tq // tk) % 2 == 0, "the chunk loop is unrolled by two"
    r = jnp.arange(tk)
    u = (r[:, None] > r[None, :]).astype(BF16)
    return pl.pallas_call(
        functools.partial(_sb_prompt_body, tq=tq, tk=tk, head_dim=head_dim),
        grid=(nseq, d // pair, nq),
        in_specs=[pl.BlockSpec(memory_space=pltpu.SMEM),
                  pl.BlockSpec((tq, pair), lambda b, p, i: (b * nq + i, p)),
                  pl.BlockSpec((seq, pair), lambda b, p, i: (b, p)),
                  pl.BlockSpec((seq, pair), lambda b, p, i: (b, p)),
                  pl.BlockSpec((tk, tk), lambda b, p, i: (0, 0))],
        out_specs=pl.BlockSpec((tq, pair), lambda b, p, i: (b * nq + i, p)),
        out_shape=jax.ShapeDtypeStruct((t, d), BF16),
        scratch_shapes=[pltpu.VMEM((2 * tq, pair), BF16), pltpu.VMEM((2 * tq, pair), F32),
                        pltpu.VMEM((2 * tq, 1), F32),
                        pltpu.VMEM((2, 2 * tq, tk), F32), pltpu.VMEM((2, 2 * tq, tk), BF16),
                        pltpu.VMEM((2, 2 * tq, tk), BF16), pltpu.VMEM((2, 2 * tq, 1), F32)],
        compiler_params=_cparams(("arbitrary", "arbitrary", "arbitrary"), 40),
        name="sb_attn_prompt",
    )(bias.astype(F32), qs, kb, vb, u)


def _sb_decode_body(pt_ref, q_ref, kn_ref, vn_ref, *rest, n_past, pps):
    kc_refs, vc_refs = rest[:pps], rest[pps:2 * pps]
    u2_ref, bias_ref, o_ref, q_scr, acc_scr, c_scr = rest[2 * pps:]
    seq = pl.program_id(0)
    j = pl.program_id(1)
    nh, hd, page = kc_refs[0].shape
    sub = SUBLANES
    bias = bias_ref[...]

    def bf(x):
        return x.astype(BF16).astype(F32)

    def column(ref):
        lane = lax.broadcasted_iota(jnp.int32, ref.shape, 1)
        col = jnp.sum(jnp.where(lane == seq, bf(ref[...]), 0.0), axis=1, keepdims=True)
        return jnp.broadcast_to(col, (nh * hd, page)).reshape(nh, hd, page)

    def logits(keys_of_head):
        rows = []
        for h in range(nh):
            part = (bf(keys_of_head(h)) * q_scr[h]).reshape(hd // sub, sub, page).sum(axis=0)
            for s in (4, 2, 1):
                part = part + pltpu.roll(part, s, 0)
            rows.append(part)
        return jnp.concatenate(rows, axis=0) + bias

    def accumulate(w, vals_of_head):
        for h in range(nh):
            wh = w[sub * h:sub * (h + 1)]
            vals = bf(vals_of_head(h)).reshape(hd // sub, sub, page)
            acc_scr[h] += (vals * wh[None]).reshape(hd, page)

    @pl.when(j == 0)
    def _():
        q_scr[...] = column(q_ref)
        kn = column(kn_ref)
        vn = column(vn_ref)
        q_pos = n_past
        k_pos = n_past
        lb, _ = _log_sigmoid_pair(logits(lambda h: kn[h]))
        w_new = bf(jnp.where(k_pos < q_pos, jnp.exp(lb), 0.0)) * (1.0 / page)
        acc_scr[...] = jnp.zeros_like(acc_scr)
        accumulate(w_new, lambda h: vn[h])
        c_scr[...] = jnp.zeros_like(c_scr)

    u2 = u2_ref[...]
    for p in range(pps):
        z = logits(lambda h: kc_refs[p][h])
        lb, l1 = _log_sigmoid_pair(z)
        hi, lo = _hi_lo(l1)
        suffix = _dot(jnp.concatenate([hi, lo], axis=1), u2)
        c = c_scr[...]
        w = bf(jnp.exp(lb + (suffix + c)))
        c_scr[...] = c + (suffix[:, 0:1] + l1[:, 0:1])
        accumulate(w, lambda h: vc_refs[p][h])

    @pl.when(j == pl.num_programs(1) - 1)
    def _():
        o_ref[...] = jnp.sum(acc_scr[...], axis=-1)


def _sb_decode_call(qs_t, k_new_t, v_new_t, cache_k, cache_v, page_table, bias, pages_per_step=16):
    d, b = qs_t.shape
    n_pages = page_table.shape[1]
    _, n_heads, head_dim, page = cache_k.shape
    pps = _tile(n_pages, pages_per_step)
    r = jnp.arange(page)
    u = (r[:, None] > r[None, :]).astype(BF16)
    u2 = jnp.concatenate([u, u], axis=0)
    bias_col = jnp.repeat(bias.astype(F32), SUBLANES).reshape(n_heads * SUBLANES, 1)

    def cache(p):
        return lambda i, j, pt: (pt[i * n_pages + (n_pages - 1 - (j * pps + p))], 0, 0, 0)

    const = lambda i, j, pt: (0, 0)
    page_specs = [pl.BlockSpec((None, n_heads, head_dim, page), cache(p)) for p in range(pps)]
    grid_spec = pltpu.PrefetchScalarGridSpec(
        num_scalar_prefetch=1,
        grid=(b, n_pages // pps),
        in_specs=[pl.BlockSpec((d, b), const), pl.BlockSpec((d, b), const), pl.BlockSpec((d, b), const)]
                 + page_specs + page_specs
                 + [pl.BlockSpec((2 * page, page), const), pl.BlockSpec((n_heads * SUBLANES, 1), const)],
        out_specs=pl.BlockSpec((None, n_heads, head_dim), lambda i, j, pt: (i, 0, 0)),
        scratch_shapes=[pltpu.VMEM((n_heads, head_dim, page), F32),
                        pltpu.VMEM((n_heads, head_dim, page), F32),
                        pltpu.VMEM((n_heads * SUBLANES, 1), F32)],
    )
    return pl.pallas_call(
        functools.partial(_sb_decode_body, n_past=n_pages * page, pps=pps),
        grid_spec=grid_spec,
        out_shape=jax.ShapeDtypeStruct((b, n_heads, head_dim), F32),
        compiler_params=_cparams(("arbitrary", "arbitrary"), 48),
        name="sb_attn_decode",
    )(page_table.reshape(-1), qs_t, k_new_t, v_new_t, *([cache_k] * pps), *([cache_v] * pps), u2, bias_col)


def _s5_constants(lam_re, lam_im, log_dt, b_re, b_im, c_re, c_im, run):
    lam_re = jnp.minimum(lam_re.astype(F32), LAMBDA_RE_MAX)
    lam_im = lam_im.astype(F32)
    dt = jnp.exp(log_dt.astype(F32))[:, None]
    decay = jnp.exp(lam_re * dt)
    a_re = decay * jnp.cos(lam_im * dt)
    a_im = decay * jnp.sin(lam_im * dt)
    inv = 1.0 / (lam_re * lam_re + lam_im * lam_im)
    f_re = ((a_re - 1.0) * lam_re + a_im * lam_im) * inv
    f_im = (a_im * lam_re - (a_re - 1.0) * lam_im) * inv
    b_re, b_im = b_re.astype(F32), b_im.astype(F32)
    bb_re = f_re[..., None] * b_re - f_im[..., None] * b_im
    bb_im = f_re[..., None] * b_im + f_im[..., None] * b_re
    g, p, c = bb_re.shape
    gl = LANES // c
    nkb = g // gl
    eye = jnp.eye(gl, dtype=F32)

    def in_map(bb):
        m = bb.transpose(0, 2, 1).reshape(nkb, gl, c, p)
        return jnp.einsum("kgcp,gh->kgchp", m, eye).reshape(nkb, gl * c, gl * p).astype(BF16)

    def out_map(cm):
        m = cm.astype(F32).transpose(0, 2, 1).reshape(nkb, gl, p, c)
        return jnp.einsum("kgpc,gh->kgphc", m, eye).reshape(nkb, gl * p, gl * c).astype(BF16)

    ar, ai = a_re.reshape(-1), a_im.reshape(-1)

    def cmul(x, y):
        return (x[0] * y[0] - x[1] * y[1], x[0] * y[1] + x[1] * y[0])

    a1 = (ar, ai)
    pw = [a1]
    for _ in range(run - 1):
        pw.append(cmul(pw[-1], a1))
    a_pow = jnp.stack([jnp.stack([q[0] for q in pw]), jnp.stack([q[1] for q in pw])])
    hop = [pw[run - 1]]
    for _ in range(2):
        hop.append(cmul(hop[-1], hop[-1]))
    rows = jnp.arange(SUBLANES)[:, None]
    hops = jnp.stack([jnp.stack([jnp.where(rows >= (1 << k), hop[k][0][None, :], 0.0),
                                 jnp.where(rows >= (1 << k), hop[k][1][None, :], 0.0)])
                      for k in range(3)])
    a_one = jnp.stack([ar, ai]).reshape(2, 1, -1)
    return in_map(bb_re), in_map(bb_im), out_map(c_re), out_map(c_im), hops, a_pow, a_one


def _glu_out(y, wg_ref, bg_ref):
    z = _dot(y.astype(BF16), wg_ref[...]) + bg_ref[...]
    d = z.shape[1] // 2
    return z[:, :d] * jax.nn.sigmoid(z[:, d:])


def _s5_prompt_body(x_ref, g_ref, sh_ref, sc_ref, gt_ref, bbr_ref, bbi_ref, cr_ref, ci_ref, ah_ref,
                    ap_ref, a_ref, perm_ref, unperm_ref, dk_ref, wg_ref, bg_ref, o_ref, fre_ref, fim_ref,
                    y_scr, sr_scr, si_scr, st_scr, *, tl):
    i = pl.program_id(1)
    nkb, kin, cw = bbr_ref.shape
    sub = SUBLANES
    run = tl // sub
    last = sub - 1

    @pl.when(i == 0)
    def _():
        st_scr[...] = jnp.zeros_like(st_scr)

    x = x_ref[...]
    h = _modnorm(x, g_ref[...], sh_ref[...], sc_ref[...])
    perm = perm_ref[...]
    hb = _dot(perm, h.astype(BF16)).astype(BF16)
    for kb in range(nkb):
        cols = slice(kb * cw, (kb + 1) * cw)
        hk = hb[:, kb * kin:(kb + 1) * kin]
        sr_scr[...] = _dot(hk, bbr_ref[kb]).reshape(run, sub, cw)
        si_scr[...] = _dot(hk, bbi_ref[kb]).reshape(run, sub, cw)
        a_r = a_ref[0, :, cols]
        a_i = a_ref[1, :, cols]
        s_r = sr_scr[0]
        s_i = si_scr[0]
        for p in range(1, run):
            s_r, s_i = sr_scr[p] + (a_r * s_r - a_i * s_i), si_scr[p] + (a_r * s_i + a_i * s_r)
            sr_scr[p] = s_r
            si_scr[p] = s_i
        row = lax.broadcasted_iota(jnp.int32, (sub, cw), 0)
        e_r = jnp.where(row == 0, st_scr[0, :, cols], pltpu.roll(s_r, 1, 0))
        e_i = jnp.where(row == 0, st_scr[1, :, cols], pltpu.roll(s_i, 1, 0))
        for k in range(3):
            h_r = ah_ref[k, 0, :, cols]
            h_i = ah_ref[k, 1, :, cols]
            p_r = pltpu.roll(e_r, 1 << k, 0)
            p_i = pltpu.roll(e_i, 1 << k, 0)
            e_r, e_i = e_r + h_r * p_r - h_i * p_i, e_i + h_r * p_i + h_i * p_r
        n_r = ap_ref[0, run - 1:run, cols]
        n_i = ap_ref[1, run - 1:run, cols]
        st_scr[0, :, cols] = s_r[last:] + (n_r * e_r[last:] - n_i * e_i[last:])
        st_scr[1, :, cols] = s_i[last:] + (n_r * e_i[last:] + n_i * e_r[last:])
        for p in range(run):
            w_r = ap_ref[0, p:p + 1, cols]
            w_i = ap_ref[1, p:p + 1, cols]
            sr_scr[p] += w_r * e_r - w_i * e_i
            si_scr[p] += w_r * e_i + w_i * e_r
        s_re = sr_scr[...].reshape(tl, cw).astype(BF16)
        s_im = si_scr[...].reshape(tl, cw).astype(BF16)
        y_scr[:, kb * kin:(kb + 1) * kin] = _dot(s_re, cr_ref[kb]) - _dot(s_im, ci_ref[kb])
    y = y_scr[...]
    y1 = y.astype(BF16)
    y2 = (y - y1.astype(F32)).astype(BF16)
    y3 = ((y - y1.astype(F32)) - y2.astype(F32)).astype(BF16)
    unperm = unperm_ref[...]
    y = (_dot(unperm, y1) + _dot(unperm, y2)) + _dot(unperm, y3)
    y = y + dk_ref[...] * h
    o_ref[...] = x + gt_ref[...] * _glu_out(y, wg_ref, bg_ref)
    fre_ref[...] = st_scr[0]
    fim_ref[...] = st_scr[1]


def _s5_prompt_call(x, gain4, mod5, consts, d_skip, w_glu_bf, b_glu, layer, nseq, tl):
    t, d = x.shape
    ni = (t // nseq) // tl
    bbr, bbi, cr, ci, hops, a_pow, a_one = consts
    assert a_pow.shape[1] == tl // SUBLANES, "constants were built for another tile length"
    nkb, kin, cw = bbr.shape
    nch = nkb * cw
    sh, sc, gt = _mod_specs(mod5, layer, 1, ni)

    def mspec(s):
        return pl.BlockSpec(s.block_shape, lambda b, i, _f=s.index_map: _f(b * ni + i))

    def whole(a):
        nd = a.ndim
        return pl.BlockSpec(a.shape, lambda b, i: (0,) * nd)

    dk = d_skip.astype(F32).reshape(1, d)
    bg = b_glu.astype(F32).reshape(1, -1)
    run = tl // SUBLANES
    src = jnp.arange(tl)
    src = (src % SUBLANES) * run + src // SUBLANES
    perm = (src[:, None] == jnp.arange(tl)[None, :]).astype(BF16)
    unperm = perm.T
    tok = lambda b, i: (b * ni + i, 0)
    x_out, fre, fim = pl.pallas_call(
        functools.partial(_s5_prompt_body, tl=tl),
        grid=(nseq, ni),
        in_specs=[pl.BlockSpec((tl, d), tok),
                  pl.BlockSpec((None, None, 1, d), lambda b, i: (layer, 1, 0, 0)),
                  mspec(sh), mspec(sc), mspec(gt),
                  whole(bbr), whole(bbi), whole(cr), whole(ci), whole(hops), whole(a_pow), whole(a_one),
                  whole(perm), whole(unperm), whole(dk), whole(w_glu_bf), whole(bg)],
        out_specs=[pl.BlockSpec((tl, d), tok),
                   pl.BlockSpec((None, 1, nch), lambda b, i: (b, 0, 0)),
                   pl.BlockSpec((None, 1, nch), lambda b, i: (b, 0, 0))],
        out_shape=[jax.ShapeDtypeStruct((t, d), F32),
                   jax.ShapeDtypeStruct((nseq, 1, nch), F32),
                   jax.ShapeDtypeStruct((nseq, 1, nch), F32)],
        scratch_shapes=[pltpu.VMEM((tl, d), F32),
                        pltpu.VMEM((tl // SUBLANES, SUBLANES, cw), F32),
                        pltpu.VMEM((tl // SUBLANES, SUBLANES, cw), F32),
                        pltpu.VMEM((2, 1, nch), F32)],
        compiler_params=_cparams(("arbitrary", "arbitrary"), 48),
        name="s5_mixer_prompt",
    )(x, gain4, mod5, mod5, mod5, bbr, bbi, cr, ci, hops, a_pow, a_one, perm, unperm, dk, w_glu_bf, bg)
    return x_out, fre.reshape(nseq, nch), fim.reshape(nseq, nch)


def _s5_step_body(x_ref, g_ref, sh_ref, sc_ref, gt_ref, pre_ref, pim_ref, bbr_ref, bbi_ref, cr_ref,
                  ci_ref, a_ref, dk_ref, wg_ref, bg_ref, o_ref, nre_ref, nim_ref, y_scr):
    nkb, kin, cw = bbr_ref.shape
    x = x_ref[...]
    h = _modnorm(x, g_ref[...], sh_ref[...], sc_ref[...])
    hb = h.astype(BF16)
    for kb in range(nkb):
        cols = slice(kb * cw, (kb + 1) * cw)
        hk = hb[:, kb * kin:(kb + 1) * kin]
        ar = a_ref[0, :, cols]
        ai = a_ref[1, :, cols]
        pr = pre_ref[:, cols]
        pi = pim_ref[:, cols]
        s_r = _dot(hk, bbr_ref[kb]) + (ar * pr - ai * pi)
        s_i = _dot(hk, bbi_ref[kb]) + (ar * pi + ai * pr)
        nre_ref[:, cols] = s_r
        nim_ref[:, cols] = s_i
        y_scr[:, kb * kin:(kb + 1) * kin] = (_dot(s_r.astype(BF16), cr_ref[kb])
                                             - _dot(s_i.astype(BF16), ci_ref[kb]))
    y = y_scr[...] + dk_ref[...] * h
    o_ref[...] = x + gt_ref[...] * _glu_out(y, wg_ref, bg_ref)


def _s5_step_call(x, gain4, mod5, prev_re, prev_im, consts, d_skip, w_glu_bf, b_glu, layer):
    t, d = x.shape
    bbr, bbi, cr, ci, _, _, a_one = consts
    sh, sc, gt = _mod_specs(mod5, layer, 1, 1)

    def whole(a):
        nd = a.ndim
        return pl.BlockSpec(a.shape, lambda i: (0,) * nd)

    dk = d_skip.astype(F32).reshape(1, d)
    bg = b_glu.astype(F32).reshape(1, -1)
    return pl.pallas_call(
        _s5_step_body,
        grid=(1,),
        in_specs=[whole(x), pl.BlockSpec((None, None, 1, d), lambda i: (layer, 1, 0, 0)),
                  sh, sc, gt, whole(prev_re), whole(prev_im),
                  whole(bbr), whole(bbi), whole(cr), whole(ci), whole(a_one),
                  whole(dk), whole(w_glu_bf), whole(bg)],
        out_specs=[whole(x), whole(prev_re), whole(prev_im)],
        out_shape=[jax.ShapeDtypeStruct((t, d), F32), jax.ShapeDtypeStruct(prev_re.shape, F32),
                   jax.ShapeDtypeStruct(prev_im.shape, F32)],
        scratch_shapes=[pltpu.VMEM((t, d), F32)],
        compiler_params=_cparams(("arbitrary",), 48),
        name="s5_mixer_step",
    )(x, gain4, mod5, mod5, mod5, prev_re, prev_im, bbr, bbi, cr, ci, a_one, dk, w_glu_bf, bg)


def _tile(n, want):
    t = min(n, want)
    while n % t:
        t //= 2
    return t


def kernel(x_prompt, x_sample, state_conv, cache_k, cache_v, state_ssm_re, state_ssm_im, page_table, c_prompt, c_sample, ln_gain, ada_w, ada_b, ffn_w13, ffn_w2, conv_w_in, conv_w, conv_w_out, attn_w_qkv, attn_q_gain, attn_k_gain, attn_logit_bias, attn_w_o, ssm_lambda_re, ssm_lambda_im, ssm_log_dt, ssm_b_re, ssm_b_im, ssm_c_re, ssm_c_im, ssm_d, ssm_w_glu, ssm_b_glu):
    bp, seq, d = x_prompt.shape
    bs, seq_s, _ = x_sample.shape
    assert seq_s == 1, "the sample trunk handles one new token per sequence"
    depth = ln_gain.shape[0]
    n_heads, head_dim = cache_k.shape[3], cache_k.shape[4]
    page = cache_k.shape[2]
    width = conv_w.shape[1]
    n_state = ssm_lambda_re.shape[1] * ssm_lambda_re.shape[2]

    rows_p = -(-bp // SUBLANES) * SUBLANES
    c_all = jnp.concatenate([c_prompt, jnp.zeros((rows_p - bp, d), F32), c_sample], axis=0)
    mod_p, mod_s = _ada_call(c_all, ada_w, ada_b, rows_p)
    mod_p = mod_p.reshape(depth, N_SUB * 3, rows_p, 1, d)
    mod_s = mod_s.reshape(depth, N_SUB * 3, 1, bs, d)
    gain4 = ln_gain.reshape(depth, N_SUB, 1, d)

    xp = x_prompt.reshape(bp * seq, d)
    xs = x_sample.reshape(bs, d)
    tm_p = _tile(seq, 512)
    tps = seq // tm_p
    w13_bf = ffn_w13.astype(BF16)
    w2_bf = ffn_w2.astype(BF16)

    outs = dict(pc=[], pk=[], pv=[], pr=[], pi=[], sc=[], sk=[], sv=[], sr=[], si=[])
    for i in range(depth):
        kind, j = i % N_MIXERS, i // N_MIXERS
        xp = _ffn_call(xp, gain4, mod_p, w13_bf, w2_bf, i, 0, 0, tm_p, tps)
        xs = _ffn_call(xs, gain4, mod_s, w13_bf, w2_bf, i, 0, 0, bs, 1)
        if kind == 0:
            w_in = conv_w_in[j].astype(BF16)
            w_out = conv_w_out[j].astype(BF16)
            prev_p = jnp.zeros((bp, width - 1, d), F32)
            xp, st = _conv_prompt_call(xp, gain4, mod_p, prev_p, w_in, conv_w[j], w_out, i, bp, tm_p)
            outs["pc"].append(st)
            xs, st = _conv_step_call(xs, gain4, mod_s, state_conv[j].reshape(bs, (width - 1) * d),
                                     w_in, conv_w[j], w_out, i)
            outs["sc"].append(st.reshape(bs, width - 1, d))
        elif kind == 1:
            w_qkv = attn_w_qkv[j].astype(BF16)
            w_o = attn_w_o[j].astype(BF16)
            qs, kb, vb, k_t, v_t = _qkv_call(xp, gain4, mod_p, w_qkv, attn_q_gain[j], attn_k_gain[j], i,
                                             bp, tm_p, head_dim, decode=False)
            outs["pk"].append(k_t.reshape(bp, n_heads, head_dim, seq).transpose(0, 3, 1, 2))
            outs["pv"].append(v_t.reshape(bp, n_heads, head_dim, seq).transpose(0, 3, 1, 2))
            o = _sb_prompt_call(qs, kb, vb, attn_logit_bias[j], bp, head_dim, _tile(seq, 512),
                                _tile(seq, 256))
            xp = _proj_res_call(xp, o, mod_p, w_o, i, tm_p, tps)
            qs_t, k_t, v_t = _qkv_call(xs, gain4, mod_s, w_qkv, attn_q_gain[j], attn_k_gain[j], i,
                                       1, bs, head_dim, decode=True)
            k_t, v_t = k_t[0], v_t[0]
            outs["sk"].append(k_t.reshape(n_heads, head_dim, bs, 1).transpose(2, 3, 0, 1))
            outs["sv"].append(v_t.reshape(n_heads, head_dim, bs, 1).transpose(2, 3, 0, 1))
            o = _sb_decode_call(qs_t, k_t, v_t, cache_k[j].transpose(0, 2, 3, 1),
                                cache_v[j].transpose(0, 2, 3, 1), page_table, attn_logit_bias[j])
            xs = _proj_res_call(xs, o.reshape(bs, d), mod_s, w_o, i, bs, 1)
        else:
            tl = _tile(seq, 256)
            consts = _s5_constants(ssm_lambda_re[j], ssm_lambda_im[j], ssm_log_dt[j], ssm_b_re[j],
                                   ssm_b_im[j], ssm_c_re[j], ssm_c_im[j], tl // SUBLANES)
            w_glu = ssm_w_glu[j].astype(BF16)
            xp, fre, fim = _s5_prompt_call(xp, gain4, mod_p, consts, ssm_d[j], w_glu, ssm_b_glu[j], i,
                                           bp, tl)
            outs["pr"].append(fre.reshape(bp, -1, ssm_lambda_re.shape[2]))
            outs["pi"].append(fim.reshape(bp, -1, ssm_lambda_re.shape[2]))
            xs, nre, nim = _s5_step_call(xs, gain4, mod_s, state_ssm_re[j].reshape(bs, n_state),
                                         state_ssm_im[j].reshape(bs, n_state), consts, ssm_d[j], w_glu,
                                         ssm_b_glu[j], i)
            outs["sr"].append(nre.reshape(state_ssm_re.shape[1:]))
            outs["si"].append(nim.reshape(state_ssm_im.shape[1:]))
        xp = _ffn_call(xp, gain4, mod_p, w13_bf, w2_bf, i, 1, 2, tm_p, tps)
        xs = _ffn_call(xs, gain4, mod_s, w13_bf, w2_bf, i, 1, 2, bs, 1)

    st = {k: jnp.stack(v) for k, v in outs.items()}
    return (xp.reshape(bp, seq, d), xs.reshape(bs, 1, d), st["pc"], st["pk"], st["pv"], st["pr"],
            st["pi"], st["sc"], st["sk"], st["sv"], st["sr"], st["si"])
```

```python
import functools

import jax
import jax.numpy as jnp
from jax import lax
from jax.experimental import pallas as pl
from jax.experimental.pallas import tpu as pltpu

F32 = jnp.float32
BF16 = jnp.bfloat16

N_MIXERS = 3
N_SUB = 3
RMS_EPS = 1e-6
FFN_RES_WEIGHT = 0.5
LAMBDA_RE_MAX = -1e-4
MASKED_LOG = -1e30
SUBLANES = 8
LANES = 128
MXU_WIDTH = 256
MIB = 1024 * 1024

TOKEN_TILE = 512
S5_TILE = 256
ATTN_Q_TILE = 512
ATTN_K_TILE = MXU_WIDTH
DECODE_PAGES_PER_STEP = 16
VMEM_MIB = {"ada_mod": 32, "ffn_swiglu": 56, "ffn_swiglu_cast": 32, "proj_residual": 32,
            "conv_mixer_prompt": 40, "conv_mixer_step": 40, "qkv_proj": 40, "sb_attn_prompt": 40,
            "sb_attn_decode": 48, "s5_mixer_prompt": 48, "s5_mixer_step": 48}


def _call_options(name, grid_rank):
    return dict(name=name,
                compiler_params=pltpu.CompilerParams(dimension_semantics=("arbitrary",) * grid_rank,
                                                     vmem_limit_bytes=VMEM_MIB[name] * MIB))


def _dot(a, b):
    return jnp.dot(a, b, preferred_element_type=F32)


def _dot_nt(a, b):
    return lax.dot_general(a, b, (((1,), (1,)), ((), ())), preferred_element_type=F32)


def _hi_lo(x):
    hi = x.astype(BF16)
    lo = (x - hi.astype(F32)).astype(BF16)
    return hi, lo


def _modnorm(x, gain, shift, scale):
    ms = jnp.mean(x * x, axis=-1, keepdims=True)
    y = x * lax.rsqrt(ms + RMS_EPS)
    return (y * gain) * (1.0 + scale) + shift


def _silu(x):
    return x * jax.nn.sigmoid(x)


def _mod_specs(mod5, layer, sub, tiles_per_seq):
    r, d = mod5.shape[3], mod5.shape[4]

    def spec(t):
        return pl.BlockSpec((None, None, None, r, d),
                            lambda i, *_: (layer, N_SUB * sub + t, i // tiles_per_seq, 0, 0))

    return spec(0), spec(1), spec(2)


def _ada_body(c_ref, w_ref, b_ref, op_ref, os_ref, *, rows_p):
    ca = _silu(c_ref[...]).astype(BF16)
    m = _dot(ca, w_ref[...].astype(BF16)) + b_ref[...]
    op_ref[...] = m[:rows_p]
    os_ref[...] = m[rows_p:]


def _ada_call(c_all, ada_w, ada_b, rows_p):
    depth, d, n = ada_w.shape
    nrow = n // d
    rows = c_all.shape[0]
    rows_s = rows - rows_p
    b4 = ada_b.reshape(depth, nrow, 1, d)
    return pl.pallas_call(
        functools.partial(_ada_body, rows_p=rows_p),
        grid=(depth, nrow),
        in_specs=[pl.BlockSpec((rows, d), lambda l, j: (0, 0)),
                  pl.BlockSpec((None, d, d), lambda l, j: (l, 0, j)),
                  pl.BlockSpec((None, None, 1, d), lambda l, j: (l, j, 0, 0))],
        out_specs=[pl.BlockSpec((None, None, rows_p, d), lambda l, j: (l, j, 0, 0)),
                   pl.BlockSpec((None, None, rows_s, d), lambda l, j: (l, j, 0, 0))],
        out_shape=[jax.ShapeDtypeStruct((depth, nrow, rows_p, d), F32),
                   jax.ShapeDtypeStruct((depth, nrow, rows_s, d), F32)],
        **_call_options("ada_mod", 2),
    )(c_all, ada_w, b4)


def _ffn_body(x_ref, g_ref, sh_ref, sc_ref, gt_ref, w1_ref, w3_ref, w2_ref, o_ref, *, tf):
    dff = w2_ref.shape[0]
    x = x_ref[...]
    h = _modnorm(x, g_ref[...], sh_ref[...], sc_ref[...]).astype(BF16)
    parts = []
    for c in range(0, dff, tf):
        g = _dot(h, w1_ref[:, c:c + tf])
        u = _dot(h, w3_ref[:, c:c + tf])
        parts.append((_silu(g) * u).astype(BF16))
    a = jnp.concatenate(parts, axis=1)
    o_ref[...] = x + (FFN_RES_WEIGHT * gt_ref[...]) * _dot(a, w2_ref[...])


def _ffn_call(x, gain4, mod5, w1_bf, w3_bf, w2_bf, layer, sub, tm, tiles_per_seq, tf=MXU_WIDTH):
    t, d = x.shape
    sh, sc, gt = _mod_specs(mod5, layer, sub, tiles_per_seq)
    once = pl.Buffered(1)
    const = lambda i: (0, 0)
    return pl.pallas_call(
        functools.partial(_ffn_body, tf=tf),
        grid=(t // tm,),
        in_specs=[pl.BlockSpec((tm, d), lambda i: (i, 0)),
                  pl.BlockSpec((None, None, 1, d), lambda i: (layer, sub, 0, 0)),
                  sh, sc, gt,
                  pl.BlockSpec(w1_bf.shape, const, pipeline_mode=once),
                  pl.BlockSpec(w3_bf.shape, const, pipeline_mode=once),
                  pl.BlockSpec(w2_bf.shape, const, pipeline_mode=once)],
        out_specs=pl.BlockSpec((tm, d), lambda i: (i, 0)),
        out_shape=jax.ShapeDtypeStruct((t, d), F32),
        **_call_options("ffn_swiglu", 1),
    )(x, gain4, mod5, mod5, mod5, w1_bf, w3_bf, w2_bf)


def _ffn_cast_body(x_ref, g_ref, sh_ref, sc_ref, gt_ref, w1_ref, w3_ref, w2_ref,
                   o_ref, w1b_ref, w3b_ref, w2b_ref, h_scr, acc_scr):
    j = pl.program_id(0)

    @pl.when(j == 0)
    def _():
        h_scr[...] = _modnorm(x_ref[...], g_ref[...], sh_ref[...], sc_ref[...]).astype(BF16)
        acc_scr[...] = jnp.zeros_like(acc_scr)

    w1 = w1_ref[...].astype(BF16)
    w3 = w3_ref[...].astype(BF16)
    w2 = w2_ref[...].astype(BF16)
    w1b_ref[...] = w1
    w3b_ref[...] = w3
    w2b_ref[...] = w2
    h = h_scr[...]
    a = (_silu(_dot(h, w1)) * _dot(h, w3)).astype(BF16)
    acc_scr[...] += _dot(a, w2)

    @pl.when(j == pl.num_programs(0) - 1)
    def _():
        o_ref[...] = x_ref[...] + (FFN_RES_WEIGHT * gt_ref[...]) * acc_scr[...]


def _ffn_cast_call(x, gain4, mod5, w13, w2, layer, which, sub, tf=MXU_WIDTH):
    t, d = x.shape
    dff = w2.shape[2]
    nf = dff // tf
    sh, sc, gt = _mod_specs(mod5, layer, sub, 1)
    fix = lambda spec: pl.BlockSpec(spec.block_shape, lambda j, _f=spec.index_map: _f(0))
    whole = lambda j: (0, 0)
    return pl.pallas_call(
        _ffn_cast_body,
        grid=(nf,),
        in_specs=[pl.BlockSpec((t, d), whole),
                  pl.BlockSpec((None, None, 1, d), lambda j: (layer, sub, 0, 0)),
                  fix(sh), fix(sc), fix(gt),
                  pl.BlockSpec((None, None, d, tf), lambda j: (layer, which, 0, j)),
                  pl.BlockSpec((None, None, d, tf), lambda j: (layer, which, 0, j + nf)),
                  pl.BlockSpec((None, None, tf, d), lambda j: (layer, which, j, 0))],
        out_specs=[pl.BlockSpec((t, d), whole),
                   pl.BlockSpec((d, tf), lambda j: (0, j)),
                   pl.BlockSpec((d, tf), lambda j: (0, j)),
                   pl.BlockSpec((tf, d), lambda j: (j, 0))],
        out_shape=[jax.ShapeDtypeStruct((t, d), F32),
                   jax.ShapeDtypeStruct((d, dff), BF16), jax.ShapeDtypeStruct((d, dff), BF16),
                   jax.ShapeDtypeStruct((dff, d), BF16)],
        scratch_shapes=[pltpu.VMEM((t, d), BF16), pltpu.VMEM((t, d), F32)],
        **_call_options("ffn_swiglu_cast", 1),
    )(x, gain4, mod5, mod5, mod5, w13, w13, w2)


def _proj_res_body(x_ref, a_ref, gt_ref, w_ref, o_ref):
    o_ref[...] = x_ref[...] + gt_ref[...] * _dot(a_ref[...].astype(BF16), w_ref[...])


def _proj_res_call(x, a, mod5, w_bf, layer, tm, tiles_per_seq):
    t, d = x.shape
    _, _, gt = _mod_specs(mod5, layer, 1, tiles_per_seq)
    return pl.pallas_call(
        _proj_res_body,
        grid=(t // tm,),
        in_specs=[pl.BlockSpec((tm, d), lambda i: (i, 0)),
                  pl.BlockSpec((tm, a.shape[1]), lambda i: (i, 0)),
                  gt,
                  pl.BlockSpec(w_bf.shape, lambda i: (0, 0))],
        out_specs=pl.BlockSpec((tm, d), lambda i: (i, 0)),
        out_shape=jax.ShapeDtypeStruct((t, d), F32),
        **_call_options("proj_residual", 1),
    )(x, a, mod5, w_bf)


def _conv_prompt_body(x_ref, g_ref, sh_ref, sc_ref, gt_ref, prev_ref, win_ref, cw_ref, wo_ref,
                      o_ref, st_ref, u_scr, *, tm, tc, width):
    i = pl.program_id(1)
    d = x_ref.shape[1]
    halo = SUBLANES
    first = halo - (width - 1)

    @pl.when(i == 0)
    def _():
        u_scr[first:halo, :] = prev_ref[...]

    x = x_ref[...]
    h = _modnorm(x, g_ref[...], sh_ref[...], sc_ref[...]).astype(BF16)
    parts = []
    for c in range(0, d, tc):
        cols = slice(c, c + tc)
        bg = _dot(h, win_ref[:, cols])
        cg = _dot(h, win_ref[:, d + c:d + c + tc])
        xi = _dot(h, win_ref[:, 2 * d + c:2 * d + c + tc])
        u_scr[halo:halo + tm, cols] = cg * xi
        conv = u_scr[first:first + tm, cols] * cw_ref[0:1, cols]
        for tap in range(1, width):
            conv = conv + u_scr[first + tap:first + tap + tm, cols] * cw_ref[tap:tap + 1, cols]
        parts.append((bg * conv).astype(BF16))
    o_ref[...] = x + gt_ref[...] * _dot(jnp.concatenate(parts, axis=1), wo_ref[...])
    tail = u_scr[tm:tm + halo, :]
    u_scr[0:halo, :] = tail
    st_ref[...] = tail[first:]


def _conv_prompt_call(x, gain4, mod5, prev, w_in_bf, conv_w, w_out_bf, layer, nseq, tm, tc=MXU_WIDTH):
    t, d = x.shape
    seq = t // nseq
    ni = seq // tm
    width = conv_w.shape[0]
    sh, sc, gt = _mod_specs(mod5, layer, 1, ni)
    once = pl.Buffered(1)
    tok = lambda b, i: (b * ni + i, 0)
    const = lambda b, i: (0, 0)

    def mspec(s):
        return pl.BlockSpec(s.block_shape, lambda b, i, _f=s.index_map: _f(b * ni + i))

    return pl.pallas_call(
        functools.partial(_conv_prompt_body, tm=tm, tc=tc, width=width),
        grid=(nseq, ni),
        in_specs=[pl.BlockSpec((tm, d), tok),
                  pl.BlockSpec((None, None, 1, d), lambda b, i: (layer, 1, 0, 0)),
                  mspec(sh), mspec(sc), mspec(gt),
                  pl.BlockSpec((None, width - 1, d), lambda b, i: (b, 0, 0)),
                  pl.BlockSpec(w_in_bf.shape, const, pipeline_mode=once),
                  pl.BlockSpec(conv_w.shape, const),
                  pl.BlockSpec(w_out_bf.shape, const, pipeline_mode=once)],
        out_specs=[pl.BlockSpec((tm, d), tok),
                   pl.BlockSpec((None, width - 1, d), lambda b, i: (b, 0, 0))],
        out_shape=[jax.ShapeDtypeStruct((t, d), F32),
                   jax.ShapeDtypeStruct((nseq, width - 1, d), F32)],
        scratch_shapes=[pltpu.VMEM((tm + SUBLANES, d), F32)],
        **_call_options("conv_mixer_prompt", 2),
    )(x, gain4, mod5, mod5, mod5, prev, w_in_bf, conv_w, w_out_bf)


def _conv_step_body(x_ref, g_ref, sh_ref, sc_ref, gt_ref, prev_ref, wb_ref, wc_ref, wx_ref, cw_ref,
                    wo_ref, o_ref, st_ref, *, width):
    d = x_ref.shape[1]
    x = x_ref[...]
    h = _modnorm(x, g_ref[...], sh_ref[...], sc_ref[...]).astype(BF16)
    bg = _dot(h, wb_ref[...])
    u = _dot(h, wc_ref[...]) * _dot(h, wx_ref[...])
    cw = cw_ref[...]
    taps = [prev_ref[:, k * d:(k + 1) * d] for k in range(width - 1)] + [u]
    conv = taps[0] * cw[0:1]
    for k in range(1, width):
        conv = conv + taps[k] * cw[k:k + 1]
    o_ref[...] = x + gt_ref[...] * _dot((bg * conv).astype(BF16), wo_ref[...])
    for k in range(width - 1):
        st_ref[:, k * d:(k + 1) * d] = taps[k + 1]


def _conv_step_call(x, gain4, mod5, prev2, w_in_bf, conv_w, w_out_bf, layer):
    t, d = x.shape
    width = conv_w.shape[0]
    sh, sc, gt = _mod_specs(mod5, layer, 1, 1)
    full = lambda i: (0, 0)
    return pl.pallas_call(
        functools.partial(_conv_step_body, width=width),
        grid=(1,),
        in_specs=[pl.BlockSpec((t, d), full),
                  pl.BlockSpec((None, None, 1, d), lambda i: (layer, 1, 0, 0)),
                  sh, sc, gt,
                  pl.BlockSpec(prev2.shape, full),
                  pl.BlockSpec((d, d), lambda i: (0, 0)),
                  pl.BlockSpec((d, d), lambda i: (0, 1)),
                  pl.BlockSpec((d, d), lambda i: (0, 2)),
                  pl.BlockSpec(conv_w.shape, full),
                  pl.BlockSpec((d, d), full)],
        out_specs=[pl.BlockSpec((t, d), full), pl.BlockSpec(prev2.shape, full)],
        out_shape=[jax.ShapeDtypeStruct((t, d), F32), jax.ShapeDtypeStruct(prev2.shape, F32)],
        **_call_options("conv_mixer_step", 1),
    )(x, gain4, mod5, mod5, mod5, prev2, w_in_bf, w_in_bf, w_in_bf, conv_w, w_out_bf)


def _qkv_body(x_ref, g_ref, sh_ref, sc_ref, w_ref, qg_ref, kg_ref, seg_ref, *rest,
              head_dim, q_scale, tn, decode):
    d = x_ref.shape[1]
    h = _modnorm(x_ref[...], g_ref[...], sh_ref[...], sc_ref[...]).astype(BF16)
    seg = seg_ref[...]

    def head_norm(y, gain):
        hi, lo = _hi_lo(y * y)
        ms = (_dot(hi, seg) + _dot(lo, seg)) * (1.0 / head_dim)
        return (y * lax.rsqrt(ms + RMS_EPS)) * gain

    for c in range(0, d, tn):
        cols = slice(c, c + tn)
        q = head_norm(_dot(h, w_ref[:, cols]), qg_ref[...]) * q_scale
        k = head_norm(_dot(h, w_ref[:, d + c:d + c + tn]), kg_ref[...])
        v = _dot(h, w_ref[:, 2 * d + c:2 * d + c + tn])
        if decode:
            qs_ref, kt_ref, vt_ref, t_scr = rest
            qs_ref[cols, :] = q.T.astype(BF16)
        else:
            qs_ref, kb_ref, vb_ref, kt_ref, vt_ref, t_scr = rest
            qs_ref[:, cols] = q.astype(BF16)
            kb_ref[:, cols] = k.astype(BF16)
            vb_ref[:, cols] = v.astype(BF16)
        kt_ref[cols, :] = k.T
        t_scr[...] = v
        vt_ref[cols, :] = t_scr[...].T


def _qkv_call(x, gain4, mod5, w_qkv_bf, q_gain, k_gain, layer, nseq, tm, head_dim, decode, tn=MXU_WIDTH):
    t, d = x.shape
    seq = t // nseq
    ni = seq // tm
    sh, sc, _ = _mod_specs(mod5, layer, 1, ni)
    reps = tn // head_dim
    qg = jnp.tile(q_gain.astype(F32), reps).reshape(1, tn)
    kg = jnp.tile(k_gain.astype(F32), reps).reshape(1, tn)
    lane_head = jnp.arange(tn) // head_dim
    seg = (lane_head[:, None] == lane_head[None, :]).astype(BF16)
    const = lambda i: (0, 0)
    tok = pl.BlockSpec((tm, d), lambda i: (i, 0))
    feat = pl.BlockSpec((None, d, tm), lambda i: (i // ni, 0, i % ni))
    feat_shape = jax.ShapeDtypeStruct((nseq, d, seq), F32)
    if decode:
        out_specs = [pl.BlockSpec((d, tm), lambda i: (0, i)), feat, feat]
        out_shape = [jax.ShapeDtypeStruct((d, t), BF16), feat_shape, feat_shape]
    else:
        out_specs = [tok, tok, tok, feat, feat]
        out_shape = [jax.ShapeDtypeStruct((t, d), BF16)] * 3 + [feat_shape, feat_shape]
    return pl.pallas_call(
        functools.partial(_qkv_body, head_dim=head_dim, q_scale=head_dim ** -0.5, tn=tn, decode=decode),
        grid=(t // tm,),
        in_specs=[tok,
                  pl.BlockSpec((None, None, 1, d), lambda i: (layer, 1, 0, 0)),
                  sh, sc,
                  pl.BlockSpec(w_qkv_bf.shape, const, pipeline_mode=pl.Buffered(1)),
                  pl.BlockSpec((1, tn), const), pl.BlockSpec((1, tn), const),
                  pl.BlockSpec((tn, tn), const)],
        out_specs=out_specs,
        out_shape=out_shape,
        scratch_shapes=[pltpu.VMEM((tm, tn), F32)],
        **_call_options("qkv_proj", 1),
    )(x, gain4, mod5, mod5, w_qkv_bf, qg, kg, seg)


def _log1p_exp_neg_abs(z):
    sign_bit = jnp.uint32(0x80000000)
    neg_abs = lax.bitcast_convert_type(lax.bitcast_convert_type(z, jnp.uint32) | sign_bit, F32)
    return jnp.log(1.0 + jnp.exp(neg_abs))


def _softplus(z):
    return jnp.maximum(z, 0.0) + _log1p_exp_neg_abs(z)


def _log_sigmoid_pair(z):
    lb = jnp.minimum(z, 0.0) - _log1p_exp_neg_abs(z)
    return lb, lb - z


def _sb_prompt_body(bias_ref, q_ref, k_ref, v_ref, u_ref, o_ref, q2_scr, o_scr, c_scr, z_scr,
                    hi_scr, lo_scr, *, tq, tk, head_dim):
    hp = pl.program_id(1)
    qi = pl.program_id(2)
    n_diag = tq // tk
    n_chunks = (qi + 1) * n_diag
    q = q_ref[...]
    lane = lax.broadcasted_iota(jnp.int32, (tk, q.shape[1]), 1)
    for blk in range(n_diag):
        q_blk = q[blk * tk:(blk + 1) * tk]
        q2_scr[2 * blk * tk:(2 * blk + 1) * tk] = jnp.where(lane < head_dim, q_blk, jnp.zeros_like(q_blk))
        q2_scr[(2 * blk + 1) * tk:(2 * blk + 2) * tk] = jnp.where(lane >= head_dim, q_blk,
                                                                 jnp.zeros_like(q_blk))
    row2 = lax.broadcasted_iota(jnp.int32, (2 * tq, 1), 0)
    second = (row2 // tk) % 2 == 1
    bias = jnp.where(second, bias_ref[2 * hp + 1], bias_ref[2 * hp])
    q_row = (row2 // (2 * tk)) * tk + row2 % tk
    u = u_ref[...]
    o_scr[...] = jnp.zeros_like(o_scr)
    c_scr[...] = jnp.zeros_like(c_scr)

    def chunk_rows(m):
        return pl.ds(pl.multiple_of((n_chunks - 1 - m) * tk, tk), tk)

    def live_rows(diag):
        return slice(0 if diag is None else 2 * diag * tk, 2 * tq)

    def stage1(m, slot, diag):
        rows = live_rows(diag)
        z = _dot_nt(q2_scr[rows], k_ref[chunk_rows(m), :]) + bias[rows]
        n1 = _softplus(z)
        if diag is not None:
            col = lax.broadcasted_iota(jnp.int32, z.shape, 1)
            mask = (col + diag * tk) < q_row[rows]
            n1 = jnp.where(mask, n1, 0.0)
            z = jnp.where(mask, z, MASKED_LOG)
        hi, lo = _hi_lo(n1)
        z_scr[slot, rows] = z
        hi_scr[slot, rows] = hi
        lo_scr[slot, rows] = lo

    def stage2(m, slot, diag):
        rows = live_rows(diag)
        suffix = _dot(hi_scr[slot, rows], u) + _dot(lo_scr[slot, rows], u)
        c = c_scr[rows]
        w = jnp.exp(z_scr[slot, rows] + (suffix + c))
        c_scr[rows] = c + suffix[:, 0:1]
        o_scr[rows] += _dot(w.astype(BF16), v_ref[chunk_rows(m), :])

    for m in range(n_diag):
        stage1(m, m % 2, n_diag - 1 - m)
        if m:
            stage2(m - 1, (m - 1) % 2, n_diag - m)

    def run(first, trips, unroll):
        def body(jj, carry):
            for k in range(unroll):
                m = first + unroll * jj + k
                stage1(m, (n_diag + k) % 2, None)
                stage2(m - 1, (n_diag + k - 1) % 2, None)
            return carry

        lax.fori_loop(0, trips, body, 0)

    rest = n_chunks - n_diag
    run(n_diag, rest // 4, 4)
    run(n_diag + (rest // 4) * 4, (rest % 4) // 2, 2)
    stage2(n_chunks - 1, (n_diag - 1) % 2, None)
    for blk in range(n_diag):
        o_ref[blk * tk:(blk + 1) * tk] = jnp.where(lane < head_dim, o_scr[2 * blk * tk:(2 * blk + 1) * tk],
                                                   o_scr[(2 * blk + 1) * tk:(2 * blk + 2) * tk]).astype(BF16)


def _sb_prompt_call(qs, kb, vb, bias, nseq, head_dim, tq, tk):
    t, d = qs.shape
    seq = t // nseq
    nq = seq // tq
    pair = 2 * head_dim
    assert (tq // tk) % 2 == 0, "the chunk loop is unrolled by two"
    r = jnp.arange(tk)
    u = -(r[:, None] >= r[None, :]).astype(BF16)
    return pl.pallas_call(
        functools.partial(_sb_prompt_body, tq=tq, tk=tk, head_dim=head_dim),
        grid=(nseq, d // pair, nq),
        in_specs=[pl.BlockSpec(memory_space=pltpu.SMEM),
                  pl.BlockSpec((tq, pair), lambda b, p, i: (b * nq + i, p)),
                  pl.BlockSpec((seq, pair), lambda b, p, i: (b, p)),
                  pl.BlockSpec((seq, pair), lambda b, p, i: (b, p)),
                  pl.BlockSpec((tk, tk), lambda b, p, i: (0, 0))],
        out_specs=pl.BlockSpec((tq, pair), lambda b, p, i: (b * nq + i, p)),
        out_shape=jax.ShapeDtypeStruct((t, d), BF16),
        scratch_shapes=[pltpu.VMEM((2 * tq, pair), BF16), pltpu.VMEM((2 * tq, pair), F32),
                        pltpu.VMEM((2 * tq, 1), F32),
                        pltpu.VMEM((2, 2 * tq, tk), F32), pltpu.VMEM((2, 2 * tq, tk), BF16),
                        pltpu.VMEM((2, 2 * tq, tk), BF16)],
        **_call_options("sb_attn_prompt", 3),
    )(bias.astype(F32), qs, kb, vb, u)


def _sb_decode_body(pt_ref, q_ref, kn_ref, vn_ref, *rest, n_past, pps):
    kc_refs, vc_refs = rest[:pps], rest[pps:2 * pps]
    u2_ref, bias_ref, o_ref, q_scr, acc_scr, c_scr = rest[2 * pps:]
    seq = pl.program_id(0)
    j = pl.program_id(1)
    nh, hd, page = kc_refs[0].shape
    sub = SUBLANES
    bias = bias_ref[...]

    def bf(x):
        return x.astype(BF16).astype(F32)

    def column(ref):
        lane = lax.broadcasted_iota(jnp.int32, ref.shape, 1)
        col = jnp.sum(jnp.where(lane == seq, bf(ref[...]), 0.0), axis=1, keepdims=True)
        return jnp.broadcast_to(col, (nh * hd, page)).reshape(nh, hd, page)

    def logits(keys_of_head):
        rows = []
        for h in range(nh):
            part = (bf(keys_of_head(h)) * q_scr[h]).reshape(hd // sub, sub, page).sum(axis=0)
            for s in (4, 2, 1):
                part = part + pltpu.roll(part, s, 0)
            rows.append(part)
        return jnp.concatenate(rows, axis=0) + bias

    def accumulate(w, vals_of_head):
        for h in range(nh):
            wh = w[sub * h:sub * (h + 1)]
            vals = bf(vals_of_head(h)).reshape(hd // sub, sub, page)
            acc_scr[h] += (vals * wh[None]).reshape(hd, page)

    @pl.when(j == 0)
    def _():
        q_scr[...] = column(q_ref)
        kn = column(kn_ref)
        vn = column(vn_ref)
        q_pos = n_past
        k_pos = n_past
        lb, _ = _log_sigmoid_pair(logits(lambda h: kn[h]))
        w_new = bf(jnp.where(k_pos < q_pos, jnp.exp(lb), 0.0)) * (1.0 / page)
        acc_scr[...] = jnp.zeros_like(acc_scr)
        accumulate(w_new, lambda h: vn[h])
        c_scr[...] = jnp.zeros_like(c_scr)

    u2 = u2_ref[...]
    for p in range(pps):
        z = logits(lambda h: kc_refs[p][h])
        lb, l1 = _log_sigmoid_pair(z)
        hi, lo = _hi_lo(l1)
        suffix = _dot(jnp.concatenate([hi, lo], axis=1), u2)
        c = c_scr[...]
        w = bf(jnp.exp(lb + (suffix + c)))
        c_scr[...] = c + (suffix[:, 0:1] + l1[:, 0:1])
        accumulate(w, lambda h: vc_refs[p][h])

    @pl.when(j == pl.num_programs(1) - 1)
    def _():
        o_ref[...] = jnp.sum(acc_scr[...], axis=-1)


def _sb_decode_call(qs_t, k_new_t, v_new_t, cache_k, cache_v, page_table, bias,
                    pages_per_step=DECODE_PAGES_PER_STEP):
    d, b = qs_t.shape
    n_pages = page_table.shape[1]
    _, n_heads, head_dim, page = cache_k.shape
    pps = _tile(n_pages, pages_per_step)
    r = jnp.arange(page)
    u = (r[:, None] > r[None, :]).astype(BF16)
    u2 = jnp.concatenate([u, u], axis=0)
    bias_col = jnp.repeat(bias.astype(F32), SUBLANES).reshape(n_heads * SUBLANES, 1)

    def cache(p):
        return lambda i, j, pt: (pt[i * n_pages + (n_pages - 1 - (j * pps + p))], 0, 0, 0)

    const = lambda i, j, pt: (0, 0)
    page_specs = [pl.BlockSpec((None, n_heads, head_dim, page), cache(p)) for p in range(pps)]
    grid_spec = pltpu.PrefetchScalarGridSpec(
        num_scalar_prefetch=1,
        grid=(b, n_pages // pps),
        in_specs=[pl.BlockSpec((d, b), const), pl.BlockSpec((d, b), const), pl.BlockSpec((d, b), const)]
                 + page_specs + page_specs
                 + [pl.BlockSpec((2 * page, page), const), pl.BlockSpec((n_heads * SUBLANES, 1), const)],
        out_specs=pl.BlockSpec((None, n_heads, head_dim), lambda i, j, pt: (i, 0, 0)),
        scratch_shapes=[pltpu.VMEM((n_heads, head_dim, page), F32),
                        pltpu.VMEM((n_heads, head_dim, page), F32),
                        pltpu.VMEM((n_heads * SUBLANES, 1), F32)],
    )
    return pl.pallas_call(
        functools.partial(_sb_decode_body, n_past=n_pages * page, pps=pps),
        grid_spec=grid_spec,
        out_shape=jax.ShapeDtypeStruct((b, n_heads, head_dim), F32),
        **_call_options("sb_attn_decode", 2),
    )(page_table.reshape(-1), qs_t, k_new_t, v_new_t, *([cache_k] * pps), *([cache_v] * pps), u2, bias_col)


def _s5_constants(lam_re, lam_im, log_dt, b_re, b_im, c_re, c_im, run):
    lam_re = jnp.minimum(lam_re.astype(F32), LAMBDA_RE_MAX)
    lam_im = lam_im.astype(F32)
    dt = jnp.exp(log_dt.astype(F32))[:, None]
    decay = jnp.exp(lam_re * dt)
    a_re = decay * jnp.cos(lam_im * dt)
    a_im = decay * jnp.sin(lam_im * dt)
    inv = 1.0 / (lam_re * lam_re + lam_im * lam_im)
    f_re = ((a_re - 1.0) * lam_re + a_im * lam_im) * inv
    f_im = (a_im * lam_re - (a_re - 1.0) * lam_im) * inv
    b_re, b_im = b_re.astype(F32), b_im.astype(F32)
    bb_re = f_re[..., None] * b_re - f_im[..., None] * b_im
    bb_im = f_re[..., None] * b_im + f_im[..., None] * b_re
    g, p, c = bb_re.shape
    gl = LANES // c
    nkb = g // gl
    eye = jnp.eye(gl, dtype=F32)

    def in_map(bb):
        m = bb.transpose(0, 2, 1).reshape(nkb, gl, c, p)
        return jnp.einsum("kgcp,gh->kgchp", m, eye).reshape(nkb, gl * c, gl * p).astype(BF16)

    def out_map(cm):
        m = cm.astype(F32).transpose(0, 2, 1).reshape(nkb, gl, p, c)
        return jnp.einsum("kgpc,gh->kgphc", m, eye).reshape(nkb, gl * p, gl * c).astype(BF16)

    ar, ai = a_re.reshape(-1), a_im.reshape(-1)

    def cmul(x, y):
        return (x[0] * y[0] - x[1] * y[1], x[0] * y[1] + x[1] * y[0])

    a1 = (ar, ai)
    pw = [a1]
    for _ in range(run - 1):
        pw.append(cmul(pw[-1], a1))
    a_pow = jnp.stack([jnp.stack([q[0] for q in pw]), jnp.stack([q[1] for q in pw])])
    hop = [pw[run - 1]]
    for _ in range(2):
        hop.append(cmul(hop[-1], hop[-1]))
    rows = jnp.arange(SUBLANES)[:, None]
    hops = jnp.stack([jnp.stack([jnp.where(rows >= (1 << k), hop[k][0][None, :], 0.0),
                                 jnp.where(rows >= (1 << k), hop[k][1][None, :], 0.0)])
                      for k in range(3)])
    a_one = jnp.stack([ar, ai]).reshape(2, 1, -1)
    return in_map(bb_re), in_map(bb_im), out_map(c_re), out_map(c_im), hops, a_pow, a_one


def _glu_out(y, wg_ref, bg_ref):
    z = _dot(y.astype(BF16), wg_ref[...]) + bg_ref[...]
    d = z.shape[1] // 2
    return z[:, :d] * jax.nn.sigmoid(z[:, d:])


def _s5_prompt_body(x_ref, g_ref, sh_ref, sc_ref, gt_ref, bbr_ref, bbi_ref, cr_ref, ci_ref, ah_ref,
                    ap_ref, a_ref, perm_ref, unperm_ref, dk_ref, wg_ref, bg_ref, o_ref, fre_ref, fim_ref,
                    y_scr, sr_scr, si_scr, st_scr, *, tl):
    i = pl.program_id(1)
    nkb, kin, cw = bbr_ref.shape
    sub = SUBLANES
    run = tl // sub
    last = sub - 1

    @pl.when(i == 0)
    def _():
        st_scr[...] = jnp.zeros_like(st_scr)

    x = x_ref[...]
    h = _modnorm(x, g_ref[...], sh_ref[...], sc_ref[...])
    perm = perm_ref[...]
    hb = _dot(perm, h.astype(BF16)).astype(BF16)
    for kb in range(nkb):
        cols = slice(kb * cw, (kb + 1) * cw)
        hk = hb[:, kb * kin:(kb + 1) * kin]
        sr_scr[...] = _dot(hk, bbr_ref[kb]).reshape(run, sub, cw)
        si_scr[...] = _dot(hk, bbi_ref[kb]).reshape(run, sub, cw)
        a_r = a_ref[0, :, cols]
        a_i = a_ref[1, :, cols]
        s_r = sr_scr[0]
        s_i = si_scr[0]
        for p in range(1, run):
            s_r, s_i = sr_scr[p] + (a_r * s_r - a_i * s_i), si_scr[p] + (a_r * s_i + a_i * s_r)
            sr_scr[p] = s_r
            si_scr[p] = s_i
        row = lax.broadcasted_iota(jnp.int32, (sub, cw), 0)
        e_r = jnp.where(row == 0, st_scr[0, :, cols], pltpu.roll(s_r, 1, 0))
        e_i = jnp.where(row == 0, st_scr[1, :, cols], pltpu.roll(s_i, 1, 0))
        for k in range(3):
            h_r = ah_ref[k, 0, :, cols]
            h_i = ah_ref[k, 1, :, cols]
            p_r = pltpu.roll(e_r, 1 << k, 0)
            p_i = pltpu.roll(e_i, 1 << k, 0)
            e_r, e_i = e_r + h_r * p_r - h_i * p_i, e_i + h_r * p_i + h_i * p_r
        n_r = ap_ref[0, run - 1:run, cols]
        n_i = ap_ref[1, run - 1:run, cols]
        st_scr[0, :, cols] = s_r[last:] + (n_r * e_r[last:] - n_i * e_i[last:])
        st_scr[1, :, cols] = s_i[last:] + (n_r * e_i[last:] + n_i * e_r[last:])
        for p in range(run):
            w_r = ap_ref[0, p:p + 1, cols]
            w_i = ap_ref[1, p:p + 1, cols]
            sr_scr[p] += w_r * e_r - w_i * e_i
            si_scr[p] += w_r * e_i + w_i * e_r
        s_re = sr_scr[...].reshape(tl, cw).astype(BF16)
        s_im = si_scr[...].reshape(tl, cw).astype(BF16)
        y_scr[:, kb * kin:(kb + 1) * kin] = _dot(s_re, cr_ref[kb]) - _dot(s_im, ci_ref[kb])
    y = y_scr[...]
    y1 = y.astype(BF16)
    y2 = (y - y1.astype(F32)).astype(BF16)
    y3 = ((y - y1.astype(F32)) - y2.astype(F32)).astype(BF16)
    unperm = unperm_ref[...]
    y = (_dot(unperm, y1) + _dot(unperm, y2)) + _dot(unperm, y3)
    y = y + dk_ref[...] * h
    o_ref[...] = x + gt_ref[...] * _glu_out(y, wg_ref, bg_ref)
    fre_ref[...] = st_scr[0]
    fim_ref[...] = st_scr[1]


def _s5_prompt_call(x, gain4, mod5, consts, d_skip, w_glu_bf, b_glu, layer, nseq, tl):
    t, d = x.shape
    ni = (t // nseq) // tl
    bbr, bbi, cr, ci, hops, a_pow, a_one = consts
    assert a_pow.shape[1] == tl // SUBLANES, "constants were built for another tile length"
    nkb, kin, cw = bbr.shape
    nch = nkb * cw
    sh, sc, gt = _mod_specs(mod5, layer, 1, ni)

    def mspec(s):
        return pl.BlockSpec(s.block_shape, lambda b, i, _f=s.index_map: _f(b * ni + i))

    def whole(a):
        nd = a.ndim
        return pl.BlockSpec(a.shape, lambda b, i: (0,) * nd)

    dk = d_skip.astype(F32).reshape(1, d)
    bg = b_glu.astype(F32).reshape(1, -1)
    run = tl // SUBLANES
    src = jnp.arange(tl)
    src = (src % SUBLANES) * run + src // SUBLANES
    perm = (src[:, None] == jnp.arange(tl)[None, :]).astype(BF16)
    unperm = perm.T
    tok = lambda b, i: (b * ni + i, 0)
    x_out, fre, fim = pl.pallas_call(
        functools.partial(_s5_prompt_body, tl=tl),
        grid=(nseq, ni),
        in_specs=[pl.BlockSpec((tl, d), tok),
                  pl.BlockSpec((None, None, 1, d), lambda b, i: (layer, 1, 0, 0)),
                  mspec(sh), mspec(sc), mspec(gt),
                  whole(bbr), whole(bbi), whole(cr), whole(ci), whole(hops), whole(a_pow), whole(a_one),
                  whole(perm), whole(unperm), whole(dk), whole(w_glu_bf), whole(bg)],
        out_specs=[pl.BlockSpec((tl, d), tok),
                   pl.BlockSpec((None, 1, nch), lambda b, i: (b, 0, 0)),
                   pl.BlockSpec((None, 1, nch), lambda b, i: (b, 0, 0))],
        out_shape=[jax.ShapeDtypeStruct((t, d), F32),
                   jax.ShapeDtypeStruct((nseq, 1, nch), F32),
                   jax.ShapeDtypeStruct((nseq, 1, nch), F32)],
        scratch_shapes=[pltpu.VMEM((tl, d), F32),
                        pltpu.VMEM((tl // SUBLANES, SUBLANES, cw), F32),
                        pltpu.VMEM((tl // SUBLANES, SUBLANES, cw), F32),
                        pltpu.VMEM((2, 1, nch), F32)],
        **_call_options("s5_mixer_prompt", 2),
    )(x, gain4, mod5, mod5, mod5, bbr, bbi, cr, ci, hops, a_pow, a_one, perm, unperm, dk, w_glu_bf, bg)
    return x_out, fre.reshape(nseq, nch), fim.reshape(nseq, nch)


def _s5_step_body(x_ref, g_ref, sh_ref, sc_ref, gt_ref, pre_ref, pim_ref, bbr_ref, bbi_ref, cr_ref,
                  ci_ref, a_ref, dk_ref, wg_ref, bg_ref, o_ref, nre_ref, nim_ref, y_scr):
    nkb, kin, cw = bbr_ref.shape
    x = x_ref[...]
    h = _modnorm(x, g_ref[...], sh_ref[...], sc_ref[...])
    hb = h.astype(BF16)
    for kb in range(nkb):
        cols = slice(kb * cw, (kb + 1) * cw)
        hk = hb[:, kb * kin:(kb + 1) * kin]
        ar = a_ref[0, :, cols]
        ai = a_ref[1, :, cols]
        pr = pre_ref[:, cols]
        pi = pim_ref[:, cols]
        s_r = _dot(hk, bbr_ref[kb]) + (ar * pr - ai * pi)
        s_i = _dot(hk, bbi_ref[kb]) + (ar * pi + ai * pr)
        nre_ref[:, cols] = s_r
        nim_ref[:, cols] = s_i
        y_scr[:, kb * kin:(kb + 1) * kin] = (_dot(s_r.astype(BF16), cr_ref[kb])
                                             - _dot(s_i.astype(BF16), ci_ref[kb]))
    y = y_scr[...] + dk_ref[...] * h
    o_ref[...] = x + gt_ref[...] * _glu_out(y, wg_ref, bg_ref)


def _s5_step_call(x, gain4, mod5, prev_re, prev_im, consts, d_skip, w_glu_bf, b_glu, layer):
    t, d = x.shape
    bbr, bbi, cr, ci, _, _, a_one = consts
    sh, sc, gt = _mod_specs(mod5, layer, 1, 1)

    def whole(a):
        nd = a.ndim
        return pl.BlockSpec(a.shape, lambda i: (0,) * nd)

    dk = d_skip.astype(F32).reshape(1, d)
    bg = b_glu.astype(F32).reshape(1, -1)
    return pl.pallas_call(
        _s5_step_body,
        grid=(1,),
        in_specs=[whole(x), pl.BlockSpec((None, None, 1, d), lambda i: (layer, 1, 0, 0)),
                  sh, sc, gt, whole(prev_re), whole(prev_im),
                  whole(bbr), whole(bbi), whole(cr), whole(ci), whole(a_one),
                  whole(dk), whole(w_glu_bf), whole(bg)],
        out_specs=[whole(x), whole(prev_re), whole(prev_im)],
        out_shape=[jax.ShapeDtypeStruct((t, d), F32), jax.ShapeDtypeStruct(prev_re.shape, F32),
                   jax.ShapeDtypeStruct(prev_im.shape, F32)],
        scratch_shapes=[pltpu.VMEM((t, d), F32)],
        **_call_options("s5_mixer_step", 1),
    )(x, gain4, mod5, mod5, mod5, prev_re, prev_im, bbr, bbi, cr, ci, a_one, dk, w_glu_bf, bg)


def _tile(n, want):
    t = min(n, want)
    while n % t:
        t //= 2
    return t


def kernel(x_prompt, x_sample, state_conv, cache_k, cache_v, state_ssm_re, state_ssm_im, page_table, c_prompt, c_sample, ln_gain, ada_w, ada_b, ffn_w13, ffn_w2, conv_w_in, conv_w, conv_w_out, attn_w_qkv, attn_q_gain, attn_k_gain, attn_logit_bias, attn_w_o, ssm_lambda_re, ssm_lambda_im, ssm_log_dt, ssm_b_re, ssm_b_im, ssm_c_re, ssm_c_im, ssm_d, ssm_w_glu, ssm_b_glu):
    bp, seq, d = x_prompt.shape
    bs, seq_s, _ = x_sample.shape
    assert seq_s == 1, "the sample trunk handles one new token per sequence"
    depth = ln_gain.shape[0]
    n_heads, head_dim = cache_k.shape[3], cache_k.shape[4]
    page = cache_k.shape[2]
    width = conv_w.shape[1]
    n_state = ssm_lambda_re.shape[1] * ssm_lambda_re.shape[2]

    rows_p = -(-bp // SUBLANES) * SUBLANES
    c_all = jnp.concatenate([c_prompt, jnp.zeros((rows_p - bp, d), F32), c_sample], axis=0)
    mod_p, mod_s = _ada_call(c_all, ada_w, ada_b, rows_p)
    mod_p = mod_p.reshape(depth, N_SUB * 3, rows_p, 1, d)
    mod_s = mod_s.reshape(depth, N_SUB * 3, 1, bs, d)
    gain4 = ln_gain.reshape(depth, N_SUB, 1, d)

    xp = x_prompt.reshape(bp * seq, d)
    xs = x_sample.reshape(bs, d)
    tm_p = _tile(seq, TOKEN_TILE)
    tps = seq // tm_p

    outs = dict(pc=[], pk=[], pv=[], pr=[], pi=[], sc=[], sk=[], sv=[], sr=[], si=[])
    for i in range(depth):
        kind, j = i % N_MIXERS, i // N_MIXERS
        xs, *w_bf = _ffn_cast_call(xs, gain4, mod_s, ffn_w13, ffn_w2, i, 0, 0)
        xp = _ffn_call(xp, gain4, mod_p, *w_bf, i, 0, tm_p, tps)
        if kind == 0:
            w_in = conv_w_in[j].astype(BF16)
            w_out = conv_w_out[j].astype(BF16)
            prev_p = jnp.zeros((bp, width - 1, d), F32)
            xp, st = _conv_prompt_call(xp, gain4, mod_p, prev_p, w_in, conv_w[j], w_out, i, bp, tm_p)
            outs["pc"].append(st)
            xs, st = _conv_step_call(xs, gain4, mod_s, state_conv[j].reshape(bs, (width - 1) * d),
                                     w_in, conv_w[j], w_out, i)
            outs["sc"].append(st.reshape(bs, width - 1, d))
        elif kind == 1:
            w_qkv = attn_w_qkv[j].astype(BF16)
            w_o = attn_w_o[j].astype(BF16)
            qs, kb, vb, k_t, v_t = _qkv_call(xp, gain4, mod_p, w_qkv, attn_q_gain[j], attn_k_gain[j], i,
                                             bp, tm_p, head_dim, decode=False)
            outs["pk"].append(k_t.reshape(bp, n_heads, head_dim, seq).transpose(0, 3, 1, 2))
            outs["pv"].append(v_t.reshape(bp, n_heads, head_dim, seq).transpose(0, 3, 1, 2))
            o = _sb_prompt_call(qs, kb, vb, attn_logit_bias[j], bp, head_dim, _tile(seq, ATTN_Q_TILE),
                                _tile(seq, ATTN_K_TILE))
            xp = _proj_res_call(xp, o, mod_p, w_o, i, tm_p, tps)
            qs_t, k_t, v_t = _qkv_call(xs, gain4, mod_s, w_qkv, attn_q_gain[j], attn_k_gain[j], i,
                                       1, bs, head_dim, decode=True)
            k_t, v_t = k_t[0], v_t[0]
            outs["sk"].append(k_t.reshape(n_heads, head_dim, bs, 1).transpose(2, 3, 0, 1))
            outs["sv"].append(v_t.reshape(n_heads, head_dim, bs, 1).transpose(2, 3, 0, 1))
            o = _sb_decode_call(qs_t, k_t, v_t, cache_k[j].transpose(0, 2, 3, 1),
                                cache_v[j].transpose(0, 2, 3, 1), page_table, attn_logit_bias[j])
            xs = _proj_res_call(xs, o.reshape(bs, d), mod_s, w_o, i, bs, 1)
        else:
            tl = _tile(seq, S5_TILE)
            consts = _s5_constants(ssm_lambda_re[j], ssm_lambda_im[j], ssm_log_dt[j], ssm_b_re[j],
                                   ssm_b_im[j], ssm_c_re[j], ssm_c_im[j], tl // SUBLANES)
            w_glu = ssm_w_glu[j].astype(BF16)
            xp, fre, fim = _s5_prompt_call(xp, gain4, mod_p, consts, ssm_d[j], w_glu, ssm_b_glu[j], i,
                                           bp, tl)
            outs["pr"].append(fre.reshape(bp, -1, ssm_lambda_re.shape[2]))
            outs["pi"].append(fim.reshape(bp, -1, ssm_lambda_re.shape[2]))
            xs, nre, nim = _s5_step_call(xs, gain4, mod_s, state_ssm_re[j].reshape(bs, n_state),
                                         state_ssm_im[j].reshape(bs, n_state), consts, ssm_d[j], w_glu,
                                         ssm_b_glu[j], i)
            outs["sr"].append(nre.reshape(state_ssm_re.shape[1:]))
            outs["si"].append(nim.reshape(state_ssm_im.shape[1:]))
        xs, *w_bf = _ffn_cast_call(xs, gain4, mod_s, ffn_w13, ffn_w2, i, 1, 2)
        xp = _ffn_call(xp, gain4, mod_p, *w_bf, i, 2, tm_p, tps)

    st = {k: jnp.stack(v) for k, v in outs.items()}
    return (xp.reshape(bp, seq, d), xs.reshape(bs, 1, d), st["pc"], st["pk"], st["pv"], st["pr"],
            st["pi"], st["sc"], st["sk"], st["sv"], st["sr"], st["si"])
```

```python
import functools

import jax
import jax.numpy as jnp
from jax import lax
from jax.experimental import pallas as pl
from jax.experimental.pallas import tpu as pltpu

F32 = jnp.float32
BF16 = jnp.bfloat16

N_MIXERS = 3
N_SUB = 3
RMS_EPS = 1e-6
FFN_RES_WEIGHT = 0.5
LAMBDA_RE_MAX = -1e-4
MASKED_LOG = -1e30
SUBLANES = 8
LANES = 128
MXU_WIDTH = 256
MIB = 1024 * 1024

TOKEN_TILE = 512
S5_TILE = 256
ATTN_Q_TILE = 1024
ATTN_K_TILE = MXU_WIDTH
DECODE_PAGES_PER_STEP = 16
VMEM_MIB = {"ada_mod": 32, "ffn_swiglu": 56, "ffn_swiglu_cast": 32, "proj_residual": 32,
            "conv_mixer_prompt": 40, "conv_mixer_step": 40, "qkv_proj": 40, "sb_attn_prompt": 40,
            "sb_attn_decode": 48, "s5_mixer_prompt": 48, "s5_mixer_step": 48}


def _call_options(name, grid_rank):
    return dict(name=name,
                compiler_params=pltpu.CompilerParams(dimension_semantics=("arbitrary",) * grid_rank,
                                                     vmem_limit_bytes=VMEM_MIB[name] * MIB))


def _dot(a, b):
    return jnp.dot(a, b, preferred_element_type=F32)


def _dot_nt(a, b):
    return lax.dot_general(a, b, (((1,), (1,)), ((), ())), preferred_element_type=F32)


def _hi_lo(x):
    hi = x.astype(BF16)
    lo = (x - hi.astype(F32)).astype(BF16)
    return hi, lo


def _modnorm(x, gain, shift, scale):
    ms = jnp.mean(x * x, axis=-1, keepdims=True)
    y = x * lax.rsqrt(ms + RMS_EPS)
    return (y * gain) * (1.0 + scale) + shift


def _silu(x):
    return x * jax.nn.sigmoid(x)


def _mod_specs(mod5, layer, sub, tiles_per_seq):
    r, d = mod5.shape[3], mod5.shape[4]

    def spec(t):
        return pl.BlockSpec((None, None, None, r, d),
                            lambda i, *_: (layer, N_SUB * sub + t, i // tiles_per_seq, 0, 0))

    return spec(0), spec(1), spec(2)


def _ada_body(c_ref, w_ref, b_ref, op_ref, os_ref, *, rows_p):
    ca = _silu(c_ref[...]).astype(BF16)
    m = _dot(ca, w_ref[...].astype(BF16)) + b_ref[...]
    op_ref[...] = m[:rows_p]
    os_ref[...] = m[rows_p:]


def _ada_call(c_all, ada_w, ada_b, rows_p):
    depth, d, n = ada_w.shape
    nrow = n // d
    rows = c_all.shape[0]
    rows_s = rows - rows_p
    b4 = ada_b.reshape(depth, nrow, 1, d)
    return pl.pallas_call(
        functools.partial(_ada_body, rows_p=rows_p),
        grid=(depth, nrow),
        in_specs=[pl.BlockSpec((rows, d), lambda l, j: (0, 0)),
                  pl.BlockSpec((None, d, d), lambda l, j: (l, 0, j)),
                  pl.BlockSpec((None, None, 1, d), lambda l, j: (l, j, 0, 0))],
        out_specs=[pl.BlockSpec((None, None, rows_p, d), lambda l, j: (l, j, 0, 0)),
                   pl.BlockSpec((None, None, rows_s, d), lambda l, j: (l, j, 0, 0))],
        out_shape=[jax.ShapeDtypeStruct((depth, nrow, rows_p, d), F32),
                   jax.ShapeDtypeStruct((depth, nrow, rows_s, d), F32)],
        **_call_options("ada_mod", 2),
    )(c_all, ada_w, b4)


def _ffn_body(x_ref, g_ref, sh_ref, sc_ref, gt_ref, w1_ref, w3_ref, w2_ref, o_ref, *, tf):
    dff = w2_ref.shape[0]
    x = x_ref[...]
    h = _modnorm(x, g_ref[...], sh_ref[...], sc_ref[...]).astype(BF16)
    parts = []
    for c in range(0, dff, tf):
        g = _dot(h, w1_ref[:, c:c + tf])
        u = _dot(h, w3_ref[:, c:c + tf])
        parts.append((_silu(g) * u).astype(BF16))
    a = jnp.concatenate(parts, axis=1)
    o_ref[...] = x + (FFN_RES_WEIGHT * gt_ref[...]) * _dot(a, w2_ref[...])


def _ffn_call(x, gain4, mod5, w1_bf, w3_bf, w2_bf, layer, sub, tm, tiles_per_seq, tf=MXU_WIDTH):
    t, d = x.shape
    sh, sc, gt = _mod_specs(mod5, layer, sub, tiles_per_seq)
    once = pl.Buffered(1)
    const = lambda i: (0, 0)
    return pl.pallas_call(
        functools.partial(_ffn_body, tf=tf),
        grid=(t // tm,),
        in_specs=[pl.BlockSpec((tm, d), lambda i: (i, 0)),
                  pl.BlockSpec((None, None, 1, d), lambda i: (layer, sub, 0, 0)),
                  sh, sc, gt,
                  pl.BlockSpec(w1_bf.shape, const, pipeline_mode=once),
                  pl.BlockSpec(w3_bf.shape, const, pipeline_mode=once),
                  pl.BlockSpec(w2_bf.shape, const, pipeline_mode=once)],
        out_specs=pl.BlockSpec((tm, d), lambda i: (i, 0)),
        out_shape=jax.ShapeDtypeStruct((t, d), F32),
        **_call_options("ffn_swiglu", 1),
    )(x, gain4, mod5, mod5, mod5, w1_bf, w3_bf, w2_bf)


def _ffn_cast_body(x_ref, g_ref, sh_ref, sc_ref, gt_ref, w1_ref, w3_ref, w2_ref,
                   o_ref, w1b_ref, w3b_ref, w2b_ref, h_scr, acc_scr):
    j = pl.program_id(0)

    @pl.when(j == 0)
    def _():
        h_scr[...] = _modnorm(x_ref[...], g_ref[...], sh_ref[...], sc_ref[...]).astype(BF16)
        acc_scr[...] = jnp.zeros_like(acc_scr)

    w1 = w1_ref[...].astype(BF16)
    w3 = w3_ref[...].astype(BF16)
    w2 = w2_ref[...].astype(BF16)
    w1b_ref[...] = w1
    w3b_ref[...] = w3
    w2b_ref[...] = w2
    h = h_scr[...]
    a = (_silu(_dot(h, w1)) * _dot(h, w3)).astype(BF16)
    acc_scr[...] += _dot(a, w2)

    @pl.when(j == pl.num_programs(0) - 1)
    def _():
        o_ref[...] = x_ref[...] + (FFN_RES_WEIGHT * gt_ref[...]) * acc_scr[...]


def _ffn_cast_call(x, gain4, mod5, w13, w2, layer, which, sub, tf=MXU_WIDTH):
    t, d = x.shape
    dff = w2.shape[2]
    nf = dff // tf
    sh, sc, gt = _mod_specs(mod5, layer, sub, 1)
    fix = lambda spec: pl.BlockSpec(spec.block_shape, lambda j, _f=spec.index_map: _f(0))
    whole = lambda j: (0, 0)
    return pl.pallas_call(
        _ffn_cast_body,
        grid=(nf,),
        in_specs=[pl.BlockSpec((t, d), whole),
                  pl.BlockSpec((None, None, 1, d), lambda j: (layer, sub, 0, 0)),
                  fix(sh), fix(sc), fix(gt),
                  pl.BlockSpec((None, None, d, tf), lambda j: (layer, which, 0, j)),
                  pl.BlockSpec((None, None, d, tf), lambda j: (layer, which, 0, j + nf)),
                  pl.BlockSpec((None, None, tf, d), lambda j: (layer, which, j, 0))],
        out_specs=[pl.BlockSpec((t, d), whole),
                   pl.BlockSpec((d, tf), lambda j: (0, j)),
                   pl.BlockSpec((d, tf), lambda j: (0, j)),
                   pl.BlockSpec((tf, d), lambda j: (j, 0))],
        out_shape=[jax.ShapeDtypeStruct((t, d), F32),
                   jax.ShapeDtypeStruct((d, dff), BF16), jax.ShapeDtypeStruct((d, dff), BF16),
                   jax.ShapeDtypeStruct((dff, d), BF16)],
        scratch_shapes=[pltpu.VMEM((t, d), BF16), pltpu.VMEM((t, d), F32)],
        **_call_options("ffn_swiglu_cast", 1),
    )(x, gain4, mod5, mod5, mod5, w13, w13, w2)


def _proj_res_body(x_ref, a_ref, gt_ref, w_ref, o_ref):
    o_ref[...] = x_ref[...] + gt_ref[...] * _dot(a_ref[...].astype(BF16), w_ref[...])


def _proj_res_call(x, a, mod5, w_bf, layer, tm, tiles_per_seq):
    t, d = x.shape
    _, _, gt = _mod_specs(mod5, layer, 1, tiles_per_seq)
    return pl.pallas_call(
        _proj_res_body,
        grid=(t // tm,),
        in_specs=[pl.BlockSpec((tm, d), lambda i: (i, 0)),
                  pl.BlockSpec((tm, a.shape[1]), lambda i: (i, 0)),
                  gt,
                  pl.BlockSpec(w_bf.shape, lambda i: (0, 0))],
        out_specs=pl.BlockSpec((tm, d), lambda i: (i, 0)),
        out_shape=jax.ShapeDtypeStruct((t, d), F32),
        **_call_options("proj_residual", 1),
    )(x, a, mod5, w_bf)


def _conv_prompt_body(x_ref, g_ref, sh_ref, sc_ref, gt_ref, prev_ref, win_ref, cw_ref, wo_ref,
                      o_ref, st_ref, u_scr, *, tm, tc, width):
    i = pl.program_id(1)
    d = x_ref.shape[1]
    halo = SUBLANES
    first = halo - (width - 1)

    @pl.when(i == 0)
    def _():
        u_scr[first:halo, :] = prev_ref[...]

    x = x_ref[...]
    h = _modnorm(x, g_ref[...], sh_ref[...], sc_ref[...]).astype(BF16)
    parts = []
    for c in range(0, d, tc):
        cols = slice(c, c + tc)
        bg = _dot(h, win_ref[:, cols])
        cg = _dot(h, win_ref[:, d + c:d + c + tc])
        xi = _dot(h, win_ref[:, 2 * d + c:2 * d + c + tc])
        u_scr[halo:halo + tm, cols] = cg * xi
        conv = u_scr[first:first + tm, cols] * cw_ref[0:1, cols]
        for tap in range(1, width):
            conv = conv + u_scr[first + tap:first + tap + tm, cols] * cw_ref[tap:tap + 1, cols]
        parts.append((bg * conv).astype(BF16))
    o_ref[...] = x + gt_ref[...] * _dot(jnp.concatenate(parts, axis=1), wo_ref[...])
    tail = u_scr[tm:tm + halo, :]
    u_scr[0:halo, :] = tail
    st_ref[...] = tail[first:]


def _conv_prompt_call(x, gain4, mod5, prev, w_in_bf, conv_w, w_out_bf, layer, nseq, tm, tc=MXU_WIDTH):
    t, d = x.shape
    seq = t // nseq
    ni = seq // tm
    width = conv_w.shape[0]
    sh, sc, gt = _mod_specs(mod5, layer, 1, ni)
    once = pl.Buffered(1)
    tok = lambda b, i: (b * ni + i, 0)
    const = lambda b, i: (0, 0)

    def mspec(s):
        return pl.BlockSpec(s.block_shape, lambda b, i, _f=s.index_map: _f(b * ni + i))

    return pl.pallas_call(
        functools.partial(_conv_prompt_body, tm=tm, tc=tc, width=width),
        grid=(nseq, ni),
        in_specs=[pl.BlockSpec((tm, d), tok),
                  pl.BlockSpec((None, None, 1, d), lambda b, i: (layer, 1, 0, 0)),
                  mspec(sh), mspec(sc), mspec(gt),
                  pl.BlockSpec((None, width - 1, d), lambda b, i: (b, 0, 0)),
                  pl.BlockSpec(w_in_bf.shape, const, pipeline_mode=once),
                  pl.BlockSpec(conv_w.shape, const),
                  pl.BlockSpec(w_out_bf.shape, const, pipeline_mode=once)],
        out_specs=[pl.BlockSpec((tm, d), tok),
                   pl.BlockSpec((None, width - 1, d), lambda b, i: (b, 0, 0))],
        out_shape=[jax.ShapeDtypeStruct((t, d), F32),
                   jax.ShapeDtypeStruct((nseq, width - 1, d), F32)],
        scratch_shapes=[pltpu.VMEM((tm + SUBLANES, d), F32)],
        **_call_options("conv_mixer_prompt", 2),
    )(x, gain4, mod5, mod5, mod5, prev, w_in_bf, conv_w, w_out_bf)


def _conv_step_body(x_ref, g_ref, sh_ref, sc_ref, gt_ref, prev_ref, wb_ref, wc_ref, wx_ref, cw_ref,
                    wo_ref, o_ref, st_ref, *, width):
    d = x_ref.shape[1]
    x = x_ref[...]
    h = _modnorm(x, g_ref[...], sh_ref[...], sc_ref[...]).astype(BF16)
    bg = _dot(h, wb_ref[...])
    u = _dot(h, wc_ref[...]) * _dot(h, wx_ref[...])
    cw = cw_ref[...]
    taps = [prev_ref[:, k * d:(k + 1) * d] for k in range(width - 1)] + [u]
    conv = taps[0] * cw[0:1]
    for k in range(1, width):
        conv = conv + taps[k] * cw[k:k + 1]
    o_ref[...] = x + gt_ref[...] * _dot((bg * conv).astype(BF16), wo_ref[...])
    for k in range(width - 1):
        st_ref[:, k * d:(k + 1) * d] = taps[k + 1]


def _conv_step_call(x, gain4, mod5, prev2, w_in_bf, conv_w, w_out_bf, layer):
    t, d = x.shape
    width = conv_w.shape[0]
    sh, sc, gt = _mod_specs(mod5, layer, 1, 1)
    full = lambda i: (0, 0)
    return pl.pallas_call(
        functools.partial(_conv_step_body, width=width),
        grid=(1,),
        in_specs=[pl.BlockSpec((t, d), full),
                  pl.BlockSpec((None, None, 1, d), lambda i: (layer, 1, 0, 0)),
                  sh, sc, gt,
                  pl.BlockSpec(prev2.shape, full),
                  pl.BlockSpec((d, d), lambda i: (0, 0)),
                  pl.BlockSpec((d, d), lambda i: (0, 1)),
                  pl.BlockSpec((d, d), lambda i: (0, 2)),
                  pl.BlockSpec(conv_w.shape, full),
                  pl.BlockSpec((d, d), full)],
        out_specs=[pl.BlockSpec((t, d), full), pl.BlockSpec(prev2.shape, full)],
        out_shape=[jax.ShapeDtypeStruct((t, d), F32), jax.ShapeDtypeStruct(prev2.shape, F32)],
        **_call_options("conv_mixer_step", 1),
    )(x, gain4, mod5, mod5, mod5, prev2, w_in_bf, w_in_bf, w_in_bf, conv_w, w_out_bf)


def _qkv_body(x_ref, g_ref, sh_ref, sc_ref, w_ref, qg_ref, kg_ref, seg_ref, *rest,
              head_dim, q_scale, tn, decode):
    d = x_ref.shape[1]
    h = _modnorm(x_ref[...], g_ref[...], sh_ref[...], sc_ref[...]).astype(BF16)
    seg = seg_ref[...]

    def head_norm(y, gain):
        hi, lo = _hi_lo(y * y)
        ms = (_dot(hi, seg) + _dot(lo, seg)) * (1.0 / head_dim)
        return (y * lax.rsqrt(ms + RMS_EPS)) * gain

    for c in range(0, d, tn):
        cols = slice(c, c + tn)
        q = head_norm(_dot(h, w_ref[:, cols]), qg_ref[...]) * q_scale
        k = head_norm(_dot(h, w_ref[:, d + c:d + c + tn]), kg_ref[...])
        v = _dot(h, w_ref[:, 2 * d + c:2 * d + c + tn])
        if decode:
            qs_ref, kt_ref, vt_ref, t_scr = rest
            qs_ref[cols, :] = q.T.astype(BF16)
        else:
            qs_ref, kb_ref, vb_ref, kt_ref, vt_ref, t_scr = rest
            qs_ref[:, cols] = q.astype(BF16)
            kb_ref[:, cols] = k.astype(BF16)
            vb_ref[:, cols] = v.astype(BF16)
        kt_ref[cols, :] = k.T
        t_scr[...] = v
        vt_ref[cols, :] = t_scr[...].T


def _qkv_call(x, gain4, mod5, w_qkv_bf, q_gain, k_gain, layer, nseq, tm, head_dim, decode, tn=MXU_WIDTH):
    t, d = x.shape
    seq = t // nseq
    ni = seq // tm
    sh, sc, _ = _mod_specs(mod5, layer, 1, ni)
    reps = tn // head_dim
    qg = jnp.tile(q_gain.astype(F32), reps).reshape(1, tn)
    kg = jnp.tile(k_gain.astype(F32), reps).reshape(1, tn)
    lane_head = jnp.arange(tn) // head_dim
    seg = (lane_head[:, None] == lane_head[None, :]).astype(BF16)
    const = lambda i: (0, 0)
    tok = pl.BlockSpec((tm, d), lambda i: (i, 0))
    feat = pl.BlockSpec((None, d, tm), lambda i: (i // ni, 0, i % ni))
    feat_shape = jax.ShapeDtypeStruct((nseq, d, seq), F32)
    if decode:
        out_specs = [pl.BlockSpec((d, tm), lambda i: (0, i)), feat, feat]
        out_shape = [jax.ShapeDtypeStruct((d, t), BF16), feat_shape, feat_shape]
    else:
        out_specs = [tok, tok, tok, feat, feat]
        out_shape = [jax.ShapeDtypeStruct((t, d), BF16)] * 3 + [feat_shape, feat_shape]
    return pl.pallas_call(
        functools.partial(_qkv_body, head_dim=head_dim, q_scale=head_dim ** -0.5, tn=tn, decode=decode),
        grid=(t // tm,),
        in_specs=[tok,
                  pl.BlockSpec((None, None, 1, d), lambda i: (layer, 1, 0, 0)),
                  sh, sc,
                  pl.BlockSpec(w_qkv_bf.shape, const, pipeline_mode=pl.Buffered(1)),
                  pl.BlockSpec((1, tn), const), pl.BlockSpec((1, tn), const),
                  pl.BlockSpec((tn, tn), const)],
        out_specs=out_specs,
        out_shape=out_shape,
        scratch_shapes=[pltpu.VMEM((tm, tn), F32)],
        **_call_options("qkv_proj", 1),
    )(x, gain4, mod5, mod5, w_qkv_bf, qg, kg, seg)


def _log1p_exp_neg_abs(z):
    sign_bit = jnp.uint32(0x80000000)
    neg_abs = lax.bitcast_convert_type(lax.bitcast_convert_type(z, jnp.uint32) | sign_bit, F32)
    return jnp.log(1.0 + jnp.exp(neg_abs))


def _softplus(z):
    return jnp.maximum(z, 0.0) + _log1p_exp_neg_abs(z)


def _log_sigmoid_pair(z):
    lb = jnp.minimum(z, 0.0) - _log1p_exp_neg_abs(z)
    return lb, lb - z


def _sb_prompt_body(bias_ref, q_ref, k_ref, v_ref, u_ref, o_ref, q2_scr, o_scr, c_scr, z_scr,
                    hi_scr, lo_scr, *, tq, tk, head_dim):
    hp = pl.program_id(1)
    qi = pl.program_id(2)
    n_diag = tq // tk
    n_chunks = (qi + 1) * n_diag
    q = q_ref[...]
    lane = lax.broadcasted_iota(jnp.int32, (tk, q.shape[1]), 1)
    for blk in range(n_diag):
        q_blk = q[blk * tk:(blk + 1) * tk]
        q2_scr[2 * blk * tk:(2 * blk + 1) * tk] = jnp.where(lane < head_dim, q_blk, jnp.zeros_like(q_blk))
        q2_scr[(2 * blk + 1) * tk:(2 * blk + 2) * tk] = jnp.where(lane >= head_dim, q_blk,
                                                                 jnp.zeros_like(q_blk))
    row2 = lax.broadcasted_iota(jnp.int32, (2 * tq, 1), 0)
    second = (row2 // tk) % 2 == 1
    bias = jnp.where(second, bias_ref[2 * hp + 1], bias_ref[2 * hp])
    q_row = (row2 // (2 * tk)) * tk + row2 % tk
    u = u_ref[...]
    o_scr[...] = jnp.zeros_like(o_scr)
    c_scr[...] = jnp.zeros_like(c_scr)

    def chunk_rows(m):
        return pl.ds(pl.multiple_of((n_chunks - 1 - m) * tk, tk), tk)

    def live_rows(diag):
        return slice(0 if diag is None else 2 * diag * tk, 2 * tq)

    def stage1(m, slot, diag):
        rows = live_rows(diag)
        z = _dot_nt(q2_scr[rows], k_ref[chunk_rows(m), :]) + bias[rows]
        n1 = _softplus(z)
        if diag is not None:
            col = lax.broadcasted_iota(jnp.int32, z.shape, 1)
            mask = (col + diag * tk) < q_row[rows]
            n1 = jnp.where(mask, n1, 0.0)
            z = jnp.where(mask, z, MASKED_LOG)
        hi, lo = _hi_lo(n1)
        z_scr[slot, rows] = z
        hi_scr[slot, rows] = hi
        lo_scr[slot, rows] = lo

    def stage2(m, slot, diag):
        rows = live_rows(diag)
        suffix = _dot(hi_scr[slot, rows], u) + _dot(lo_scr[slot, rows], u)
        c = c_scr[rows]
        w = jnp.exp(z_scr[slot, rows] + (suffix + c))
        c_scr[rows] = c + suffix[:, 0:1]
        o_scr[rows] += _dot(w.astype(BF16), v_ref[chunk_rows(m), :])

    for m in range(n_diag):
        stage1(m, m % 2, n_diag - 1 - m)
        if m:
            stage2(m - 1, (m - 1) % 2, n_diag - m)

    def run(first, trips, unroll):
        def body(jj, carry):
            for k in range(unroll):
                m = first + unroll * jj + k
                stage1(m, (n_diag + k) % 2, None)
                stage2(m - 1, (n_diag + k - 1) % 2, None)
            return carry

        lax.fori_loop(0, trips, body, 0)

    rest = n_chunks - n_diag
    run(n_diag, rest // 4, 4)
    run(n_diag + (rest // 4) * 4, (rest % 4) // 2, 2)
    stage2(n_chunks - 1, (n_diag - 1) % 2, None)
    for blk in range(n_diag):
        o_ref[blk * tk:(blk + 1) * tk] = jnp.where(lane < head_dim, o_scr[2 * blk * tk:(2 * blk + 1) * tk],
                                                   o_scr[(2 * blk + 1) * tk:(2 * blk + 2) * tk]).astype(BF16)


def _sb_prompt_call(qs, kb, vb, bias, nseq, head_dim, tq, tk):
    t, d = qs.shape
    seq = t // nseq
    nq = seq // tq
    pair = 2 * head_dim
    assert (tq // tk) % 2 == 0, "the chunk loop is unrolled by two"
    r = jnp.arange(tk)
    u = -(r[:, None] >= r[None, :]).astype(BF16)
    return pl.pallas_call(
        functools.partial(_sb_prompt_body, tq=tq, tk=tk, head_dim=head_dim),
        grid=(nseq, d // pair, nq),
        in_specs=[pl.BlockSpec(memory_space=pltpu.SMEM),
                  pl.BlockSpec((tq, pair), lambda b, p, i: (b * nq + i, p)),
                  pl.BlockSpec((seq, pair), lambda b, p, i: (b, p)),
                  pl.BlockSpec((seq, pair), lambda b, p, i: (b, p)),
                  pl.BlockSpec((tk, tk), lambda b, p, i: (0, 0))],
        out_specs=pl.BlockSpec((tq, pair), lambda b, p, i: (b * nq + i, p)),
        out_shape=jax.ShapeDtypeStruct((t, d), BF16),
        scratch_shapes=[pltpu.VMEM((2 * tq, pair), BF16), pltpu.VMEM((2 * tq, pair), F32),
                        pltpu.VMEM((2 * tq, 1), F32),
                        pltpu.VMEM((2, 2 * tq, tk), F32), pltpu.VMEM((2, 2 * tq, tk), BF16),
                        pltpu.VMEM((2, 2 * tq, tk), BF16)],
        **_call_options("sb_attn_prompt", 3),
    )(bias.astype(F32), qs, kb, vb, u)


def _sb_decode_body(pt_ref, q_ref, kn_ref, vn_ref, *rest, n_past, pps):
    kc_refs, vc_refs = rest[:pps], rest[pps:2 * pps]
    u2_ref, bias_ref, o_ref, q_scr, acc_scr, c_scr = rest[2 * pps:]
    seq = pl.program_id(0)
    j = pl.program_id(1)
    nh, hd, page = kc_refs[0].shape
    sub = SUBLANES
    bias = bias_ref[...]

    def bf(x):
        return x.astype(BF16).astype(F32)

    def column(ref):
        lane = lax.broadcasted_iota(jnp.int32, ref.shape, 1)
        col = jnp.sum(jnp.where(lane == seq, bf(ref[...]), 0.0), axis=1, keepdims=True)
        return jnp.broadcast_to(col, (nh * hd, page)).reshape(nh, hd, page)

    def logits(keys_of_head):
        rows = []
        for h in range(nh):
            part = (bf(keys_of_head(h)) * q_scr[h]).reshape(hd // sub, sub, page).sum(axis=0)
            for s in (4, 2, 1):
                part = part + pltpu.roll(part, s, 0)
            rows.append(part)
        return jnp.concatenate(rows, axis=0) + bias

    def accumulate(w, vals_of_head):
        for h in range(nh):
            wh = w[sub * h:sub * (h + 1)]
            vals = bf(vals_of_head(h)).reshape(hd // sub, sub, page)
            acc_scr[h] += (vals * wh[None]).reshape(hd, page)

    @pl.when(j == 0)
    def _():
        q_scr[...] = column(q_ref)
        kn = column(kn_ref)
        vn = column(vn_ref)
        q_pos = n_past
        k_pos = n_past
        lb, _ = _log_sigmoid_pair(logits(lambda h: kn[h]))
        w_new = bf(jnp.where(k_pos < q_pos, jnp.exp(lb), 0.0)) * (1.0 / page)
        acc_scr[...] = jnp.zeros_like(acc_scr)
        accumulate(w_new, lambda h: vn[h])
        c_scr[...] = jnp.zeros_like(c_scr)

    u2 = u2_ref[...]
    for p in range(pps):
        z = logits(lambda h: kc_refs[p][h])
        lb, l1 = _log_sigmoid_pair(z)
        hi, lo = _hi_lo(l1)
        suffix = _dot(jnp.concatenate([hi, lo], axis=1), u2)
        c = c_scr[...]
        w = bf(jnp.exp(lb + (suffix + c)))
        c_scr[...] = c + (suffix[:, 0:1] + l1[:, 0:1])
        accumulate(w, lambda h: vc_refs[p][h])

    @pl.when(j == pl.num_programs(1) - 1)
    def _():
        o_ref[...] = jnp.sum(acc_scr[...], axis=-1)


def _sb_decode_call(qs_t, k_new_t, v_new_t, cache_k, cache_v, page_table, bias,
                    pages_per_step=DECODE_PAGES_PER_STEP):
    d, b = qs_t.shape
    n_pages = page_table.shape[1]
    _, n_heads, head_dim, page = cache_k.shape
    pps = _tile(n_pages, pages_per_step)
    r = jnp.arange(page)
    u = (r[:, None] > r[None, :]).astype(BF16)
    u2 = jnp.concatenate([u, u], axis=0)
    bias_col = jnp.repeat(bias.astype(F32), SUBLANES).reshape(n_heads * SUBLANES, 1)

    def cache(p):
        return lambda i, j, pt: (pt[i * n_pages + (n_pages - 1 - (j * pps + p))], 0, 0, 0)

    const = lambda i, j, pt: (0, 0)
    page_specs = [pl.BlockSpec((None, n_heads, head_dim, page), cache(p)) for p in range(pps)]
    grid_spec = pltpu.PrefetchScalarGridSpec(
        num_scalar_prefetch=1,
        grid=(b, n_pages // pps),
        in_specs=[pl.BlockSpec((d, b), const), pl.BlockSpec((d, b), const), pl.BlockSpec((d, b), const)]
                 + page_specs + page_specs
                 + [pl.BlockSpec((2 * page, page), const), pl.BlockSpec((n_heads * SUBLANES, 1), const)],
        out_specs=pl.BlockSpec((None, n_heads, head_dim), lambda i, j, pt: (i, 0, 0)),
        scratch_shapes=[pltpu.VMEM((n_heads, head_dim, page), F32),
                        pltpu.VMEM((n_heads, head_dim, page), F32),
                        pltpu.VMEM((n_heads * SUBLANES, 1), F32)],
    )
    return pl.pallas_call(
        functools.partial(_sb_decode_body, n_past=n_pages * page, pps=pps),
        grid_spec=grid_spec,
        out_shape=jax.ShapeDtypeStruct((b, n_heads, head_dim), F32),
        **_call_options("sb_attn_decode", 2),
    )(page_table.reshape(-1), qs_t, k_new_t, v_new_t, *([cache_k] * pps), *([cache_v] * pps), u2, bias_col)


def _s5_constants(lam_re, lam_im, log_dt, b_re, b_im, c_re, c_im, run):
    lam_re = jnp.minimum(lam_re.astype(F32), LAMBDA_RE_MAX)
    lam_im = lam_im.astype(F32)
    dt = jnp.exp(log_dt.astype(F32))[:, None]
    decay = jnp.exp(lam_re * dt)
    a_re = decay * jnp.cos(lam_im * dt)
    a_im = decay * jnp.sin(lam_im * dt)
    inv = 1.0 / (lam_re * lam_re + lam_im * lam_im)
    f_re = ((a_re - 1.0) * lam_re + a_im * lam_im) * inv
    f_im = (a_im * lam_re - (a_re - 1.0) * lam_im) * inv
    b_re, b_im = b_re.astype(F32), b_im.astype(F32)
    bb_re = f_re[..., None] * b_re - f_im[..., None] * b_im
    bb_im = f_re[..., None] * b_im + f_im[..., None] * b_re
    g, p, c = bb_re.shape
    gl = MXU_WIDTH // c
    nkb = g // gl
    eye = jnp.eye(gl, dtype=F32)

    def in_map(bb):
        m = bb.transpose(0, 2, 1).reshape(nkb, gl, c, p)
        return jnp.einsum("kgcp,gh->kgchp", m, eye).reshape(nkb, gl * c, gl * p).astype(BF16)

    def out_map(cm):
        m = cm.astype(F32).transpose(0, 2, 1).reshape(nkb, gl, p, c)
        return jnp.einsum("kgpc,gh->kgphc", m, eye).reshape(nkb, gl * p, gl * c).astype(BF16)

    ar, ai = a_re.reshape(-1), a_im.reshape(-1)

    def cmul(x, y):
        return (x[0] * y[0] - x[1] * y[1], x[0] * y[1] + x[1] * y[0])

    a1 = (ar, ai)
    pw = [a1]
    for _ in range(run - 1):
        pw.append(cmul(pw[-1], a1))
    a_pow = jnp.stack([jnp.stack([q[0] for q in pw]), jnp.stack([q[1] for q in pw])])
    hop = [pw[run - 1]]
    for _ in range(2):
        hop.append(cmul(hop[-1], hop[-1]))
    rows = jnp.arange(SUBLANES)[:, None]
    hops = jnp.stack([jnp.stack([jnp.where(rows >= (1 << k), hop[k][0][None, :], 0.0),
                                 jnp.where(rows >= (1 << k), hop[k][1][None, :], 0.0)])
                      for k in range(3)])
    a_one = jnp.stack([ar, ai]).reshape(2, 1, -1)
    return in_map(bb_re), in_map(bb_im), out_map(c_re), out_map(c_im), hops, a_pow, a_one


def _glu_out(y, wg_ref, bg_ref):
    z = _dot(y.astype(BF16), wg_ref[...]) + bg_ref[...]
    d = z.shape[1] // 2
    return z[:, :d] * jax.nn.sigmoid(z[:, d:])


def _s5_prompt_body(x_ref, g_ref, sh_ref, sc_ref, gt_ref, bbr_ref, bbi_ref, cr_ref, ci_ref, ah_ref,
                    ap_ref, a_ref, perm_ref, unperm_ref, dk_ref, wg_ref, bg_ref, o_ref, fre_ref, fim_ref,
                    y_scr, sr_scr, si_scr, st_scr, *, tl):
    i = pl.program_id(1)
    nkb, kin, cw = bbr_ref.shape
    sub = SUBLANES
    run = tl // sub
    last = sub - 1

    @pl.when(i == 0)
    def _():
        st_scr[...] = jnp.zeros_like(st_scr)

    x = x_ref[...]
    h = _modnorm(x, g_ref[...], sh_ref[...], sc_ref[...])
    perm = perm_ref[...]
    hb = _dot(perm, h.astype(BF16)).astype(BF16)
    for kb in range(nkb):
        cols = slice(kb * cw, (kb + 1) * cw)
        hk = hb[:, kb * kin:(kb + 1) * kin]
        sr_scr[...] = _dot(hk, bbr_ref[kb]).reshape(run, sub, cw)
        si_scr[...] = _dot(hk, bbi_ref[kb]).reshape(run, sub, cw)
        a_r = a_ref[0, :, cols]
        a_i = a_ref[1, :, cols]
        s_r = sr_scr[0]
        s_i = si_scr[0]
        for p in range(1, run):
            s_r, s_i = sr_scr[p] + (a_r * s_r - a_i * s_i), si_scr[p] + (a_r * s_i + a_i * s_r)
            sr_scr[p] = s_r
            si_scr[p] = s_i
        row = lax.broadcasted_iota(jnp.int32, (sub, cw), 0)
        e_r = jnp.where(row == 0, st_scr[0, :, cols], pltpu.roll(s_r, 1, 0))
        e_i = jnp.where(row == 0, st_scr[1, :, cols], pltpu.roll(s_i, 1, 0))
        for k in range(3):
            h_r = ah_ref[k, 0, :, cols]
            h_i = ah_ref[k, 1, :, cols]
            p_r = pltpu.roll(e_r, 1 << k, 0)
            p_i = pltpu.roll(e_i, 1 << k, 0)
            e_r, e_i = e_r + h_r * p_r - h_i * p_i, e_i + h_r * p_i + h_i * p_r
        n_r = ap_ref[0, run - 1:run, cols]
        n_i = ap_ref[1, run - 1:run, cols]
        st_scr[0, :, cols] = s_r[last:] + (n_r * e_r[last:] - n_i * e_i[last:])
        st_scr[1, :, cols] = s_i[last:] + (n_r * e_i[last:] + n_i * e_r[last:])
        for p in range(run):
            w_r = ap_ref[0, p:p + 1, cols]
            w_i = ap_ref[1, p:p + 1, cols]
            sr_scr[p] += w_r * e_r - w_i * e_i
            si_scr[p] += w_r * e_i + w_i * e_r
        s_re = sr_scr[...].reshape(tl, cw).astype(BF16)
        s_im = si_scr[...].reshape(tl, cw).astype(BF16)
        y_scr[:, kb * kin:(kb + 1) * kin] = _dot(s_re, cr_ref[kb]) - _dot(s_im, ci_ref[kb])
    y = y_scr[...]
    y1 = y.astype(BF16)
    y2 = (y - y1.astype(F32)).astype(BF16)
    y3 = ((y - y1.astype(F32)) - y2.astype(F32)).astype(BF16)
    unperm = unperm_ref[...]
    y = (_dot(unperm, y1) + _dot(unperm, y2)) + _dot(unperm, y3)
    y = y + dk_ref[...] * h
    o_ref[...] = x + gt_ref[...] * _glu_out(y, wg_ref, bg_ref)
    fre_ref[...] = st_scr[0]
    fim_ref[...] = st_scr[1]


def _s5_prompt_call(x, gain4, mod5, consts, d_skip, w_glu_bf, b_glu, layer, nseq, tl):
    t, d = x.shape
    ni = (t // nseq) // tl
    bbr, bbi, cr, ci, hops, a_pow, a_one = consts
    assert a_pow.shape[1] == tl // SUBLANES, "constants were built for another tile length"
    nkb, kin, cw = bbr.shape
    nch = nkb * cw
    sh, sc, gt = _mod_specs(mod5, layer, 1, ni)

    def mspec(s):
        return pl.BlockSpec(s.block_shape, lambda b, i, _f=s.index_map: _f(b * ni + i))

    def whole(a):
        nd = a.ndim
        return pl.BlockSpec(a.shape, lambda b, i: (0,) * nd, pipeline_mode=pl.Buffered(1))

    dk = d_skip.astype(F32).reshape(1, d)
    bg = b_glu.astype(F32).reshape(1, -1)
    run = tl // SUBLANES
    src = jnp.arange(tl)
    src = (src % SUBLANES) * run + src // SUBLANES
    perm = (src[:, None] == jnp.arange(tl)[None, :]).astype(BF16)
    unperm = perm.T
    tok = lambda b, i: (b * ni + i, 0)
    x_out, fre, fim = pl.pallas_call(
        functools.partial(_s5_prompt_body, tl=tl),
        grid=(nseq, ni),
        in_specs=[pl.BlockSpec((tl, d), tok),
                  pl.BlockSpec((None, None, 1, d), lambda b, i: (layer, 1, 0, 0)),
                  mspec(sh), mspec(sc), mspec(gt),
                  whole(bbr), whole(bbi), whole(cr), whole(ci), whole(hops), whole(a_pow), whole(a_one),
                  whole(perm), whole(unperm), whole(dk), whole(w_glu_bf), whole(bg)],
        out_specs=[pl.BlockSpec((tl, d), tok),
                   pl.BlockSpec((None, 1, nch), lambda b, i: (b, 0, 0)),
                   pl.BlockSpec((None, 1, nch), lambda b, i: (b, 0, 0))],
        out_shape=[jax.ShapeDtypeStruct((t, d), F32),
                   jax.ShapeDtypeStruct((nseq, 1, nch), F32),
                   jax.ShapeDtypeStruct((nseq, 1, nch), F32)],
        scratch_shapes=[pltpu.VMEM((tl, d), F32),
                        pltpu.VMEM((tl // SUBLANES, SUBLANES, cw), F32),
                        pltpu.VMEM((tl // SUBLANES, SUBLANES, cw), F32),
                        pltpu.VMEM((2, 1, nch), F32)],
        **_call_options("s5_mixer_prompt", 2),
    )(x, gain4, mod5, mod5, mod5, bbr, bbi, cr, ci, hops, a_pow, a_one, perm, unperm, dk, w_glu_bf, bg)
    return x_out, fre.reshape(nseq, nch), fim.reshape(nseq, nch)


def _s5_step_body(x_ref, g_ref, sh_ref, sc_ref, gt_ref, pre_ref, pim_ref, bbr_ref, bbi_ref, cr_ref,
                  ci_ref, a_ref, dk_ref, wg_ref, bg_ref, o_ref, nre_ref, nim_ref, y_scr):
    nkb, kin, cw = bbr_ref.shape
    x = x_ref[...]
    h = _modnorm(x, g_ref[...], sh_ref[...], sc_ref[...])
    hb = h.astype(BF16)
    for kb in range(nkb):
        cols = slice(kb * cw, (kb + 1) * cw)
        hk = hb[:, kb * kin:(kb + 1) * kin]
        ar = a_ref[0, :, cols]
        ai = a_ref[1, :, cols]
        pr = pre_ref[:, cols]
        pi = pim_ref[:, cols]
        s_r = _dot(hk, bbr_ref[kb]) + (ar * pr - ai * pi)
        s_i = _dot(hk, bbi_ref[kb]) + (ar * pi + ai * pr)
        nre_ref[:, cols] = s_r
        nim_ref[:, cols] = s_i
        y_scr[:, kb * kin:(kb + 1) * kin] = (_dot(s_r.astype(BF16), cr_ref[kb])
                                             - _dot(s_i.astype(BF16), ci_ref[kb]))
    y = y_scr[...] + dk_ref[...] * h
    o_ref[...] = x + gt_ref[...] * _glu_out(y, wg_ref, bg_ref)


def _s5_step_call(x, gain4, mod5, prev_re, prev_im, consts, d_skip, w_glu_bf, b_glu, layer):
    t, d = x.shape
    bbr, bbi, cr, ci, _, _, a_one = consts
    sh, sc, gt = _mod_specs(mod5, layer, 1, 1)

    def whole(a):
        nd = a.ndim
        return pl.BlockSpec(a.shape, lambda i: (0,) * nd)

    dk = d_skip.astype(F32).reshape(1, d)
    bg = b_glu.astype(F32).reshape(1, -1)
    return pl.pallas_call(
        _s5_step_body,
        grid=(1,),
        in_specs=[whole(x), pl.BlockSpec((None, None, 1, d), lambda i: (layer, 1, 0, 0)),
                  sh, sc, gt, whole(prev_re), whole(prev_im),
                  whole(bbr), whole(bbi), whole(cr), whole(ci), whole(a_one),
                  whole(dk), whole(w_glu_bf), whole(bg)],
        out_specs=[whole(x), whole(prev_re), whole(prev_im)],
        out_shape=[jax.ShapeDtypeStruct((t, d), F32), jax.ShapeDtypeStruct(prev_re.shape, F32),
                   jax.ShapeDtypeStruct(prev_im.shape, F32)],
        scratch_shapes=[pltpu.VMEM((t, d), F32)],
        **_call_options("s5_mixer_step", 1),
    )(x, gain4, mod5, mod5, mod5, prev_re, prev_im, bbr, bbi, cr, ci, a_one, dk, w_glu_bf, bg)


def _tile(n, want):
    t = min(n, want)
    while n % t:
        t //= 2
    return t


def kernel(x_prompt, x_sample, state_conv, cache_k, cache_v, state_ssm_re, state_ssm_im, page_table, c_prompt, c_sample, ln_gain, ada_w, ada_b, ffn_w13, ffn_w2, conv_w_in, conv_w, conv_w_out, attn_w_qkv, attn_q_gain, attn_k_gain, attn_logit_bias, attn_w_o, ssm_lambda_re, ssm_lambda_im, ssm_log_dt, ssm_b_re, ssm_b_im, ssm_c_re, ssm_c_im, ssm_d, ssm_w_glu, ssm_b_glu):
    bp, seq, d = x_prompt.shape
    bs, seq_s, _ = x_sample.shape
    assert seq_s == 1, "the sample trunk handles one new token per sequence"
    depth = ln_gain.shape[0]
    n_heads, head_dim = cache_k.shape[3], cache_k.shape[4]
    page = cache_k.shape[2]
    width = conv_w.shape[1]
    n_state = ssm_lambda_re.shape[1] * ssm_lambda_re.shape[2]

    rows_p = -(-bp // SUBLANES) * SUBLANES
    c_all = jnp.concatenate([c_prompt, jnp.zeros((rows_p - bp, d), F32), c_sample], axis=0)
    mod_p, mod_s = _ada_call(c_all, ada_w, ada_b, rows_p)
    mod_p = mod_p.reshape(depth, N_SUB * 3, rows_p, 1, d)
    mod_s = mod_s.reshape(depth, N_SUB * 3, 1, bs, d)
    gain4 = ln_gain.reshape(depth, N_SUB, 1, d)

    xp = x_prompt.reshape(bp * seq, d)
    xs = x_sample.reshape(bs, d)
    tm_p = _tile(seq, TOKEN_TILE)
    tps = seq // tm_p

    outs = dict(pc=[], pk=[], pv=[], pr=[], pi=[], sc=[], sk=[], sv=[], sr=[], si=[])
    for i in range(depth):
        kind, j = i % N_MIXERS, i // N_MIXERS
        xs, *w_bf = _ffn_cast_call(xs, gain4, mod_s, ffn_w13, ffn_w2, i, 0, 0)
        xp = _ffn_call(xp, gain4, mod_p, *w_bf, i, 0, tm_p, tps)
        if kind == 0:
            w_in = conv_w_in[j].astype(BF16)
            w_out = conv_w_out[j].astype(BF16)
            prev_p = jnp.zeros((bp, width - 1, d), F32)
            xp, st = _conv_prompt_call(xp, gain4, mod_p, prev_p, w_in, conv_w[j], w_out, i, bp, tm_p)
            outs["pc"].append(st)
            xs, st = _conv_step_call(xs, gain4, mod_s, state_conv[j].reshape(bs, (width - 1) * d),
                                     w_in, conv_w[j], w_out, i)
            outs["sc"].append(st.reshape(bs, width - 1, d))
        elif kind == 1:
            w_qkv = attn_w_qkv[j].astype(BF16)
            w_o = attn_w_o[j].astype(BF16)
            qs, kb, vb, k_t, v_t = _qkv_call(xp, gain4, mod_p, w_qkv, attn_q_gain[j], attn_k_gain[j], i,
                                             bp, tm_p, head_dim, decode=False)
            outs["pk"].append(k_t.reshape(bp, n_heads, head_dim, seq).transpose(0, 3, 1, 2))
            outs["pv"].append(v_t.reshape(bp, n_heads, head_dim, seq).transpose(0, 3, 1, 2))
            o = _sb_prompt_call(qs, kb, vb, attn_logit_bias[j], bp, head_dim, _tile(seq, ATTN_Q_TILE),
                                _tile(seq, ATTN_K_TILE))
            xp = _proj_res_call(xp, o, mod_p, w_o, i, tm_p, tps)
            qs_t, k_t, v_t = _qkv_call(xs, gain4, mod_s, w_qkv, attn_q_gain[j], attn_k_gain[j], i,
                                       1, bs, head_dim, decode=True)
            k_t, v_t = k_t[0], v_t[0]
            outs["sk"].append(k_t.reshape(n_heads, head_dim, bs, 1).transpose(2, 3, 0, 1))
            outs["sv"].append(v_t.reshape(n_heads, head_dim, bs, 1).transpose(2, 3, 0, 1))
            o = _sb_decode_call(qs_t, k_t, v_t, cache_k[j].transpose(0, 2, 3, 1),
                                cache_v[j].transpose(0, 2, 3, 1), page_table, attn_logit_bias[j])
            xs = _proj_res_call(xs, o.reshape(bs, d), mod_s, w_o, i, bs, 1)
        else:
            tl = _tile(seq, S5_TILE)
            consts = _s5_constants(ssm_lambda_re[j], ssm_lambda_im[j], ssm_log_dt[j], ssm_b_re[j],
                                   ssm_b_im[j], ssm_c_re[j], ssm_c_im[j], tl // SUBLANES)
            w_glu = ssm_w_glu[j].astype(BF16)
            xp, fre, fim = _s5_prompt_call(xp, gain4, mod_p, consts, ssm_d[j], w_glu, ssm_b_glu[j], i,
                                           bp, tl)
            outs["pr"].append(fre.reshape(bp, -1, ssm_lambda_re.shape[2]))
            outs["pi"].append(fim.reshape(bp, -1, ssm_lambda_re.shape[2]))
            xs, nre, nim = _s5_step_call(xs, gain4, mod_s, state_ssm_re[j].reshape(bs, n_state),
                                         state_ssm_im[j].reshape(bs, n_state), consts, ssm_d[j], w_glu,
                                         ssm_b_glu[j], i)
            outs["sr"].append(nre.reshape(state_ssm_re.shape[1:]))
            outs["si"].append(nim.reshape(state_ssm_im.shape[1:]))
        xs, *w_bf = _ffn_cast_call(xs, gain4, mod_s, ffn_w13, ffn_w2, i, 1, 2)
        xp = _ffn_call(xp, gain4, mod_p, *w_bf, i, 2, tm_p, tps)

    st = {k: jnp.stack(v) for k, v in outs.items()}
    return (xp.reshape(bp, seq, d), xs.reshape(bs, 1, d), st["pc"], st["pk"], st["pv"], st["pr"],
            st["pi"], st["sc"], st["sk"], st["sv"], st["sr"], st["si"])
```

```python
import functools

import jax
import jax.numpy as jnp
from jax import lax
from jax.experimental import pallas as pl
from jax.experimental.pallas import tpu as pltpu

F32 = jnp.float32
BF16 = jnp.bfloat16

N_MIXERS = 3
N_SUB = 3
RMS_EPS = 1e-6
FFN_RES_WEIGHT = 0.5
LAMBDA_RE_MAX = -1e-4
MASKED_LOG = -1e30
SUBLANES = 8
LANES = 128
MXU_WIDTH = 256
MIB = 1024 * 1024

TOKEN_TILE = 512
S5_TILE = 256
ATTN_Q_TILE = 1024
ATTN_K_TILE = MXU_WIDTH
DECODE_PAGES_PER_STEP = 16
VMEM_MIB = {"ada_mod": 32, "ffn_swiglu": 56, "ffn_swiglu_cast": 32, "proj_residual": 32,
            "conv_mixer_prompt": 40, "conv_mixer_step": 40, "qkv_proj": 40, "sb_attn_prompt": 40,
            "sb_attn_decode": 48, "s5_mixer_prompt": 48, "s5_mixer_step": 48}


def _call_options(name, grid_rank):
    return dict(name=name,
                compiler_params=pltpu.CompilerParams(dimension_semantics=("arbitrary",) * grid_rank,
                                                     vmem_limit_bytes=VMEM_MIB[name] * MIB))


def _dot(a, b):
    return jnp.dot(a, b, preferred_element_type=F32)


def _dot_nt(a, b):
    return lax.dot_general(a, b, (((1,), (1,)), ((), ())), preferred_element_type=F32)


def _hi_lo(x):
    hi = x.astype(BF16)
    lo = (x - hi.astype(F32)).astype(BF16)
    return hi, lo


def _modnorm(x, gain, shift, scale):
    ms = jnp.mean(x * x, axis=-1, keepdims=True)
    y = x * lax.rsqrt(ms + RMS_EPS)
    return (y * gain) * (1.0 + scale) + shift


def _silu(x):
    return x * jax.nn.sigmoid(x)


def _mod_specs(mod5, layer, sub, tiles_per_seq):
    r, d = mod5.shape[3], mod5.shape[4]

    def spec(t):
        return pl.BlockSpec((None, None, None, r, d),
                            lambda i, *_: (layer, N_SUB * sub + t, i // tiles_per_seq, 0, 0))

    return spec(0), spec(1), spec(2)


def _ada_body(c_ref, w_ref, b_ref, op_ref, os_ref, *, rows_p):
    ca = _silu(c_ref[...]).astype(BF16)
    m = _dot(ca, w_ref[...].astype(BF16)) + b_ref[...]
    op_ref[...] = m[:rows_p]
    os_ref[...] = m[rows_p:]


def _ada_call(c_all, ada_w, ada_b, rows_p):
    depth, d, n = ada_w.shape
    nrow = n // d
    rows = c_all.shape[0]
    rows_s = rows - rows_p
    b4 = ada_b.reshape(depth, nrow, 1, d)
    return pl.pallas_call(
        functools.partial(_ada_body, rows_p=rows_p),
        grid=(depth, nrow),
        in_specs=[pl.BlockSpec((rows, d), lambda l, j: (0, 0)),
                  pl.BlockSpec((None, d, d), lambda l, j: (l, 0, j)),
                  pl.BlockSpec((None, None, 1, d), lambda l, j: (l, j, 0, 0))],
        out_specs=[pl.BlockSpec((None, None, rows_p, d), lambda l, j: (l, j, 0, 0)),
                   pl.BlockSpec((None, None, rows_s, d), lambda l, j: (l, j, 0, 0))],
        out_shape=[jax.ShapeDtypeStruct((depth, nrow, rows_p, d), F32),
                   jax.ShapeDtypeStruct((depth, nrow, rows_s, d), F32)],
        **_call_options("ada_mod", 2),
    )(c_all, ada_w, b4)


def _ffn_body(x_ref, g_ref, sh_ref, sc_ref, gt_ref, w1_ref, w3_ref, w2_ref, o_ref, *, tf):
    dff = w2_ref.shape[0]
    x = x_ref[...]
    h = _modnorm(x, g_ref[...], sh_ref[...], sc_ref[...]).astype(BF16)
    parts = []
    for c in range(0, dff, tf):
        g = _dot(h, w1_ref[:, c:c + tf])
        u = _dot(h, w3_ref[:, c:c + tf])
        parts.append((_silu(g) * u).astype(BF16))
    a = jnp.concatenate(parts, axis=1)
    o_ref[...] = x + (FFN_RES_WEIGHT * gt_ref[...]) * _dot(a, w2_ref[...])


def _ffn_call(x, gain4, mod5, w1_bf, w3_bf, w2_bf, layer, sub, tm, tiles_per_seq, tf=MXU_WIDTH):
    t, d = x.shape
    sh, sc, gt = _mod_specs(mod5, layer, sub, tiles_per_seq)
    once = pl.Buffered(1)
    const = lambda i: (0, 0)
    return pl.pallas_call(
        functools.partial(_ffn_body, tf=tf),
        grid=(t // tm,),
        in_specs=[pl.BlockSpec((tm, d), lambda i: (i, 0)),
                  pl.BlockSpec((None, None, 1, d), lambda i: (layer, sub, 0, 0)),
                  sh, sc, gt,
                  pl.BlockSpec(w1_bf.shape, const, pipeline_mode=once),
                  pl.BlockSpec(w3_bf.shape, const, pipeline_mode=once),
                  pl.BlockSpec(w2_bf.shape, const, pipeline_mode=once)],
        out_specs=pl.BlockSpec((tm, d), lambda i: (i, 0)),
        out_shape=jax.ShapeDtypeStruct((t, d), F32),
        **_call_options("ffn_swiglu", 1),
    )(x, gain4, mod5, mod5, mod5, w1_bf, w3_bf, w2_bf)


def _ffn_cast_body(x_ref, g_ref, sh_ref, sc_ref, gt_ref, w1_ref, w3_ref, w2_ref,
                   o_ref, w1b_ref, w3b_ref, w2b_ref, h_scr, acc_scr):
    j = pl.program_id(0)

    @pl.when(j == 0)
    def _():
        h_scr[...] = _modnorm(x_ref[...], g_ref[...], sh_ref[...], sc_ref[...]).astype(BF16)
        acc_scr[...] = jnp.zeros_like(acc_scr)

    w1 = w1_ref[...].astype(BF16)
    w3 = w3_ref[...].astype(BF16)
    w2 = w2_ref[...].astype(BF16)
    w1b_ref[...] = w1
    w3b_ref[...] = w3
    w2b_ref[...] = w2
    h = h_scr[...]
    a = (_silu(_dot(h, w1)) * _dot(h, w3)).astype(BF16)
    acc_scr[...] += _dot(a, w2)

    @pl.when(j == pl.num_programs(0) - 1)
    def _():
        o_ref[...] = x_ref[...] + (FFN_RES_WEIGHT * gt_ref[...]) * acc_scr[...]


def _ffn_cast_call(x, gain4, mod5, w13, w2, layer, which, sub, tf=MXU_WIDTH):
    t, d = x.shape
    dff = w2.shape[2]
    nf = dff // tf
    sh, sc, gt = _mod_specs(mod5, layer, sub, 1)
    fix = lambda spec: pl.BlockSpec(spec.block_shape, lambda j, _f=spec.index_map: _f(0))
    whole = lambda j: (0, 0)
    return pl.pallas_call(
        _ffn_cast_body,
        grid=(nf,),
        in_specs=[pl.BlockSpec((t, d), whole),
                  pl.BlockSpec((None, None, 1, d), lambda j: (layer, sub, 0, 0)),
                  fix(sh), fix(sc), fix(gt),
                  pl.BlockSpec((None, None, d, tf), lambda j: (layer, which, 0, j)),
                  pl.BlockSpec((None, None, d, tf), lambda j: (layer, which, 0, j + nf)),
                  pl.BlockSpec((None, None, tf, d), lambda j: (layer, which, j, 0))],
        out_specs=[pl.BlockSpec((t, d), whole),
                   pl.BlockSpec((d, tf), lambda j: (0, j)),
                   pl.BlockSpec((d, tf), lambda j: (0, j)),
                   pl.BlockSpec((tf, d), lambda j: (j, 0))],
        out_shape=[jax.ShapeDtypeStruct((t, d), F32),
                   jax.ShapeDtypeStruct((d, dff), BF16), jax.ShapeDtypeStruct((d, dff), BF16),
                   jax.ShapeDtypeStruct((dff, d), BF16)],
        scratch_shapes=[pltpu.VMEM((t, d), BF16), pltpu.VMEM((t, d), F32)],
        **_call_options("ffn_swiglu_cast", 1),
    )(x, gain4, mod5, mod5, mod5, w13, w13, w2)


def _proj_res_body(x_ref, a_ref, gt_ref, w_ref, o_ref):
    o_ref[...] = x_ref[...] + gt_ref[...] * _dot(a_ref[...].astype(BF16), w_ref[...])


def _proj_res_call(x, a, mod5, w_bf, layer, tm, tiles_per_seq):
    t, d = x.shape
    _, _, gt = _mod_specs(mod5, layer, 1, tiles_per_seq)
    return pl.pallas_call(
        _proj_res_body,
        grid=(t // tm,),
        in_specs=[pl.BlockSpec((tm, d), lambda i: (i, 0)),
                  pl.BlockSpec((tm, a.shape[1]), lambda i: (i, 0)),
                  gt,
                  pl.BlockSpec(w_bf.shape, lambda i: (0, 0))],
        out_specs=pl.BlockSpec((tm, d), lambda i: (i, 0)),
        out_shape=jax.ShapeDtypeStruct((t, d), F32),
        **_call_options("proj_residual", 1),
    )(x, a, mod5, w_bf)


def _conv_prompt_body(x_ref, g_ref, sh_ref, sc_ref, gt_ref, prev_ref, win_ref, cw_ref, wo_ref,
                      o_ref, st_ref, u_scr, *, tm, tc, width):
    i = pl.program_id(1)
    d = x_ref.shape[1]
    halo = SUBLANES
    first = halo - (width - 1)

    @pl.when(i == 0)
    def _():
        u_scr[first:halo, :] = prev_ref[...]

    x = x_ref[...]
    h = _modnorm(x, g_ref[...], sh_ref[...], sc_ref[...]).astype(BF16)
    parts = []
    for c in range(0, d, tc):
        cols = slice(c, c + tc)
        bg = _dot(h, win_ref[:, cols])
        cg = _dot(h, win_ref[:, d + c:d + c + tc])
        xi = _dot(h, win_ref[:, 2 * d + c:2 * d + c + tc])
        u_scr[halo:halo + tm, cols] = cg * xi
        conv = u_scr[first:first + tm, cols] * cw_ref[0:1, cols]
        for tap in range(1, width):
            conv = conv + u_scr[first + tap:first + tap + tm, cols] * cw_ref[tap:tap + 1, cols]
        parts.append((bg * conv).astype(BF16))
    o_ref[...] = x + gt_ref[...] * _dot(jnp.concatenate(parts, axis=1), wo_ref[...])
    tail = u_scr[tm:tm + halo, :]
    u_scr[0:halo, :] = tail
    st_ref[...] = tail[first:]


def _conv_prompt_call(x, gain4, mod5, prev, w_in_bf, conv_w, w_out_bf, layer, nseq, tm, tc=MXU_WIDTH):
    t, d = x.shape
    seq = t // nseq
    ni = seq // tm
    width = conv_w.shape[0]
    sh, sc, gt = _mod_specs(mod5, layer, 1, ni)
    once = pl.Buffered(1)
    tok = lambda b, i: (b * ni + i, 0)
    const = lambda b, i: (0, 0)

    def mspec(s):
        return pl.BlockSpec(s.block_shape, lambda b, i, _f=s.index_map: _f(b * ni + i))

    return pl.pallas_call(
        functools.partial(_conv_prompt_body, tm=tm, tc=tc, width=width),
        grid=(nseq, ni),
        in_specs=[pl.BlockSpec((tm, d), tok),
                  pl.BlockSpec((None, None, 1, d), lambda b, i: (layer, 1, 0, 0)),
                  mspec(sh), mspec(sc), mspec(gt),
                  pl.BlockSpec((None, width - 1, d), lambda b, i: (b, 0, 0)),
                  pl.BlockSpec(w_in_bf.shape, const, pipeline_mode=once),
                  pl.BlockSpec(conv_w.shape, const),
                  pl.BlockSpec(w_out_bf.shape, const, pipeline_mode=once)],
        out_specs=[pl.BlockSpec((tm, d), tok),
                   pl.BlockSpec((None, width - 1, d), lambda b, i: (b, 0, 0))],
        out_shape=[jax.ShapeDtypeStruct((t, d), F32),
                   jax.ShapeDtypeStruct((nseq, width - 1, d), F32)],
        scratch_shapes=[pltpu.VMEM((tm + SUBLANES, d), F32)],
        **_call_options("conv_mixer_prompt", 2),
    )(x, gain4, mod5, mod5, mod5, prev, w_in_bf, conv_w, w_out_bf)


def _conv_step_body(x_ref, g_ref, sh_ref, sc_ref, gt_ref, prev_ref, wb_ref, wc_ref, wx_ref, cw_ref,
                    wo_ref, o_ref, st_ref, *, width):
    d = x_ref.shape[1]
    x = x_ref[...]
    h = _modnorm(x, g_ref[...], sh_ref[...], sc_ref[...]).astype(BF16)
    bg = _dot(h, wb_ref[...])
    u = _dot(h, wc_ref[...]) * _dot(h, wx_ref[...])
    cw = cw_ref[...]
    taps = [prev_ref[:, k * d:(k + 1) * d] for k in range(width - 1)] + [u]
    conv = taps[0] * cw[0:1]
    for k in range(1, width):
        conv = conv + taps[k] * cw[k:k + 1]
    o_ref[...] = x + gt_ref[...] * _dot((bg * conv).astype(BF16), wo_ref[...])
    for k in range(width - 1):
        st_ref[:, k * d:(k + 1) * d] = taps[k + 1]


def _conv_step_call(x, gain4, mod5, prev2, w_in_bf, conv_w, w_out_bf, layer):
    t, d = x.shape
    width = conv_w.shape[0]
    sh, sc, gt = _mod_specs(mod5, layer, 1, 1)
    full = lambda i: (0, 0)
    return pl.pallas_call(
        functools.partial(_conv_step_body, width=width),
        grid=(1,),
        in_specs=[pl.BlockSpec((t, d), full),
                  pl.BlockSpec((None, None, 1, d), lambda i: (layer, 1, 0, 0)),
                  sh, sc, gt,
                  pl.BlockSpec(prev2.shape, full),
                  pl.BlockSpec((d, d), lambda i: (0, 0)),
                  pl.BlockSpec((d, d), lambda i: (0, 1)),
                  pl.BlockSpec((d, d), lambda i: (0, 2)),
                  pl.BlockSpec(conv_w.shape, full),
                  pl.BlockSpec((d, d), full)],
        out_specs=[pl.BlockSpec((t, d), full), pl.BlockSpec(prev2.shape, full)],
        out_shape=[jax.ShapeDtypeStruct((t, d), F32), jax.ShapeDtypeStruct(prev2.shape, F32)],
        **_call_options("conv_mixer_step", 1),
    )(x, gain4, mod5, mod5, mod5, prev2, w_in_bf, w_in_bf, w_in_bf, conv_w, w_out_bf)


def _qkv_body(x_ref, g_ref, sh_ref, sc_ref, w_ref, qg_ref, kg_ref, seg_ref, *rest,
              head_dim, q_scale, tn, decode):
    d = x_ref.shape[1]
    h = _modnorm(x_ref[...], g_ref[...], sh_ref[...], sc_ref[...]).astype(BF16)
    seg = seg_ref[...]

    def head_norm(y, gain):
        hi, lo = _hi_lo(y * y)
        ms = (_dot(hi, seg) + _dot(lo, seg)) * (1.0 / head_dim)
        return (y * lax.rsqrt(ms + RMS_EPS)) * gain

    for c in range(0, d, tn):
        cols = slice(c, c + tn)
        q = head_norm(_dot(h, w_ref[:, cols]), qg_ref[...]) * q_scale
        k = head_norm(_dot(h, w_ref[:, d + c:d + c + tn]), kg_ref[...])
        v = _dot(h, w_ref[:, 2 * d + c:2 * d + c + tn])
        if decode:
            qs_ref, kt_ref, vt_ref, t_scr = rest
            qs_ref[cols, :] = q.T.astype(BF16)
        else:
            qs_ref, kb_ref, vb_ref, kt_ref, vt_ref, t_scr = rest
            qs_ref[:, cols] = q.astype(BF16)
            kb_ref[:, cols] = k.astype(BF16)
            vb_ref[:, cols] = v.astype(BF16)
        kt_ref[cols, :] = k.T
        t_scr[...] = v
        vt_ref[cols, :] = t_scr[...].T


def _qkv_call(x, gain4, mod5, w_qkv_bf, q_gain, k_gain, layer, nseq, tm, head_dim, decode, tn=MXU_WIDTH):
    t, d = x.shape
    seq = t // nseq
    ni = seq // tm
    sh, sc, _ = _mod_specs(mod5, layer, 1, ni)
    reps = tn // head_dim
    qg = jnp.tile(q_gain.astype(F32), reps).reshape(1, tn)
    kg = jnp.tile(k_gain.astype(F32), reps).reshape(1, tn)
    lane_head = jnp.arange(tn) // head_dim
    seg = (lane_head[:, None] == lane_head[None, :]).astype(BF16)
    const = lambda i: (0, 0)
    tok = pl.BlockSpec((tm, d), lambda i: (i, 0))
    feat = pl.BlockSpec((None, d, tm), lambda i: (i // ni, 0, i % ni))
    feat_shape = jax.ShapeDtypeStruct((nseq, d, seq), F32)
    if decode:
        out_specs = [pl.BlockSpec((d, tm), lambda i: (0, i)), feat, feat]
        out_shape = [jax.ShapeDtypeStruct((d, t), BF16), feat_shape, feat_shape]
    else:
        out_specs = [tok, tok, tok, feat, feat]
        out_shape = [jax.ShapeDtypeStruct((t, d), BF16)] * 3 + [feat_shape, feat_shape]
    return pl.pallas_call(
        functools.partial(_qkv_body, head_dim=head_dim, q_scale=head_dim ** -0.5, tn=tn, decode=decode),
        grid=(t // tm,),
        in_specs=[tok,
                  pl.BlockSpec((None, None, 1, d), lambda i: (layer, 1, 0, 0)),
                  sh, sc,
                  pl.BlockSpec(w_qkv_bf.shape, const, pipeline_mode=pl.Buffered(1)),
                  pl.BlockSpec((1, tn), const), pl.BlockSpec((1, tn), const),
                  pl.BlockSpec((tn, tn), const)],
        out_specs=out_specs,
        out_shape=out_shape,
        scratch_shapes=[pltpu.VMEM((tm, tn), F32)],
        **_call_options("qkv_proj", 1),
    )(x, gain4, mod5, mod5, w_qkv_bf, qg, kg, seg)


def _log1p_exp_neg_abs(z):
    sign_bit = jnp.uint32(0x80000000)
    neg_abs = lax.bitcast_convert_type(lax.bitcast_convert_type(z, jnp.uint32) | sign_bit, F32)
    return jnp.log(1.0 + jnp.exp(neg_abs))


def _softplus(z):
    return jnp.maximum(z, 0.0) + _log1p_exp_neg_abs(z)


def _log_sigmoid_pair(z):
    lb = jnp.minimum(z, 0.0) - _log1p_exp_neg_abs(z)
    return lb, lb - z


def _sb_prompt_body(bias_ref, q_ref, k_ref, v_ref, u_ref, o_ref, q2_scr, o_scr, c_scr, z_scr,
                    hi_scr, lo_scr, *, tq, tk, head_dim):
    hp = pl.program_id(1)
    qi = pl.program_id(2)
    n_diag = tq // tk
    n_chunks = (qi + 1) * n_diag
    q = q_ref[...]
    lane = lax.broadcasted_iota(jnp.int32, (tk, q.shape[1]), 1)
    for blk in range(n_diag):
        q_blk = q[blk * tk:(blk + 1) * tk]
        q2_scr[2 * blk * tk:(2 * blk + 1) * tk] = jnp.where(lane < head_dim, q_blk, jnp.zeros_like(q_blk))
        q2_scr[(2 * blk + 1) * tk:(2 * blk + 2) * tk] = jnp.where(lane >= head_dim, q_blk,
                                                                 jnp.zeros_like(q_blk))
    row2 = lax.broadcasted_iota(jnp.int32, (2 * tq, 1), 0)
    second = (row2 // tk) % 2 == 1
    bias = jnp.where(second, bias_ref[2 * hp + 1], bias_ref[2 * hp])
    q_row = (row2 // (2 * tk)) * tk + row2 % tk
    u = u_ref[...]
    o_scr[...] = jnp.zeros_like(o_scr)
    c_scr[...] = jnp.zeros_like(c_scr)

    def chunk_rows(m):
        return pl.ds(pl.multiple_of((n_chunks - 1 - m) * tk, tk), tk)

    def live_rows(diag):
        return slice(0 if diag is None else 2 * diag * tk, 2 * tq)

    def stage1(m, slot, diag):
        rows = live_rows(diag)
        z = _dot_nt(q2_scr[rows], k_ref[chunk_rows(m), :]) + bias[rows]
        n1 = _softplus(z)

        def keep(part, z_part, n1_part):
            hi, lo = _hi_lo(n1_part)
            z_scr[slot, part] = z_part
            hi_scr[slot, part] = hi
            lo_scr[slot, part] = lo

        if diag is None:
            keep(rows, z, n1)
        else:
            edge = 2 * tk
            first = slice(rows.start, rows.start + edge)
            col = lax.broadcasted_iota(jnp.int32, (edge, tk), 1)
            mask = (col + diag * tk) < q_row[first]
            keep(first, jnp.where(mask, z[:edge], MASKED_LOG), jnp.where(mask, n1[:edge], 0.0))
            if rows.start + edge < rows.stop:
                keep(slice(rows.start + edge, rows.stop), z[edge:], n1[edge:])

    def stage2(m, slot, diag):
        rows = live_rows(diag)
        suffix = _dot(hi_scr[slot, rows], u) + _dot(lo_scr[slot, rows], u)
        c = c_scr[rows]
        w = jnp.exp(z_scr[slot, rows] + (suffix + c))
        c_scr[rows] = c + suffix[:, 0:1]
        o_scr[rows] += _dot(w.astype(BF16), v_ref[chunk_rows(m), :])

    for m in range(n_diag):
        stage1(m, m % 2, n_diag - 1 - m)
        if m:
            stage2(m - 1, (m - 1) % 2, n_diag - m)

    def run(first, trips, unroll):
        def body(jj, carry):
            for k in range(unroll):
                m = first + unroll * jj + k
                stage1(m, (n_diag + k) % 2, None)
                stage2(m - 1, (n_diag + k - 1) % 2, None)
            return carry

        lax.fori_loop(0, trips, body, 0)

    rest = n_chunks - n_diag
    run(n_diag, rest // 4, 4)
    run(n_diag + (rest // 4) * 4, (rest % 4) // 2, 2)
    stage2(n_chunks - 1, (n_diag - 1) % 2, None)
    for blk in range(n_diag):
        o_ref[blk * tk:(blk + 1) * tk] = jnp.where(lane < head_dim, o_scr[2 * blk * tk:(2 * blk + 1) * tk],
                                                   o_scr[(2 * blk + 1) * tk:(2 * blk + 2) * tk]).astype(BF16)


def _sb_prompt_call(qs, kb, vb, bias, nseq, head_dim, tq, tk):
    t, d = qs.shape
    seq = t // nseq
    nq = seq // tq
    pair = 2 * head_dim
    assert (tq // tk) % 2 == 0, "the chunk loop is unrolled by two"
    r = jnp.arange(tk)
    u = -(r[:, None] >= r[None, :]).astype(BF16)
    return pl.pallas_call(
        functools.partial(_sb_prompt_body, tq=tq, tk=tk, head_dim=head_dim),
        grid=(nseq, d // pair, nq),
        in_specs=[pl.BlockSpec(memory_space=pltpu.SMEM),
                  pl.BlockSpec((tq, pair), lambda b, p, i: (b * nq + i, p)),
                  pl.BlockSpec((seq, pair), lambda b, p, i: (b, p)),
                  pl.BlockSpec((seq, pair), lambda b, p, i: (b, p)),
                  pl.BlockSpec((tk, tk), lambda b, p, i: (0, 0))],
        out_specs=pl.BlockSpec((tq, pair), lambda b, p, i: (b * nq + i, p)),
        out_shape=jax.ShapeDtypeStruct((t, d), BF16),
        scratch_shapes=[pltpu.VMEM((2 * tq, pair), BF16), pltpu.VMEM((2 * tq, pair), F32),
                        pltpu.VMEM((2 * tq, 1), F32),
                        pltpu.VMEM((2, 2 * tq, tk), F32), pltpu.VMEM((2, 2 * tq, tk), BF16),
                        pltpu.VMEM((2, 2 * tq, tk), BF16)],
        **_call_options("sb_attn_prompt", 3),
    )(bias.astype(F32), qs, kb, vb, u)


def _sb_decode_body(pt_ref, q_ref, kn_ref, vn_ref, *rest, n_past, pps):
    kc_refs, vc_refs = rest[:pps], rest[pps:2 * pps]
    u2_ref, bias_ref, o_ref, q_scr, acc_scr, c_scr = rest[2 * pps:]
    seq = pl.program_id(0)
    j = pl.program_id(1)
    nh, hd, page = kc_refs[0].shape
    sub = SUBLANES
    bias = bias_ref[...]

    def bf(x):
        return x.astype(BF16).astype(F32)

    def column(ref):
        lane = lax.broadcasted_iota(jnp.int32, ref.shape, 1)
        col = jnp.sum(jnp.where(lane == seq, bf(ref[...]), 0.0), axis=1, keepdims=True)
        return jnp.broadcast_to(col, (nh * hd, page)).reshape(nh, hd, page)

    def logits(keys_of_head):
        rows = []
        for h in range(nh):
            part = (bf(keys_of_head(h)) * q_scr[h]).reshape(hd // sub, sub, page).sum(axis=0)
            for s in (4, 2, 1):
                part = part + pltpu.roll(part, s, 0)
            rows.append(part)
        return jnp.concatenate(rows, axis=0) + bias

    def accumulate(w, vals_of_head):
        for h in range(nh):
            wh = w[sub * h:sub * (h + 1)]
            vals = bf(vals_of_head(h)).reshape(hd // sub, sub, page)
            acc_scr[h] += (vals * wh[None]).reshape(hd, page)

    @pl.when(j == 0)
    def _():
        q_scr[...] = column(q_ref)
        kn = column(kn_ref)
        vn = column(vn_ref)
        q_pos = n_past
        k_pos = n_past
        lb, _ = _log_sigmoid_pair(logits(lambda h: kn[h]))
        w_new = bf(jnp.where(k_pos < q_pos, jnp.exp(lb), 0.0)) * (1.0 / page)
        acc_scr[...] = jnp.zeros_like(acc_scr)
        accumulate(w_new, lambda h: vn[h])
        c_scr[...] = jnp.zeros_like(c_scr)

    u2 = u2_ref[...]
    for p in range(pps):
        z = logits(lambda h: kc_refs[p][h])
        lb, l1 = _log_sigmoid_pair(z)
        hi, lo = _hi_lo(l1)
        suffix = _dot(jnp.concatenate([hi, lo], axis=1), u2)
        c = c_scr[...]
        w = bf(jnp.exp(lb + (suffix + c)))
        c_scr[...] = c + (suffix[:, 0:1] + l1[:, 0:1])
        accumulate(w, lambda h: vc_refs[p][h])

    @pl.when(j == pl.num_programs(1) - 1)
    def _():
        o_ref[...] = jnp.sum(acc_scr[...], axis=-1)


def _sb_decode_call(qs_t, k_new_t, v_new_t, cache_k, cache_v, page_table, bias,
                    pages_per_step=DECODE_PAGES_PER_STEP):
    d, b = qs_t.shape
    n_pages = page_table.shape[1]
    _, n_heads, head_dim, page = cache_k.shape
    pps = _tile(n_pages, pages_per_step)
    r = jnp.arange(page)
    u = (r[:, None] > r[None, :]).astype(BF16)
    u2 = jnp.concatenate([u, u], axis=0)
    bias_col = jnp.repeat(bias.astype(F32), SUBLANES).reshape(n_heads * SUBLANES, 1)

    def cache(p):
        return lambda i, j, pt: (pt[i * n_pages + (n_pages - 1 - (j * pps + p))], 0, 0, 0)

    const = lambda i, j, pt: (0, 0)
    page_specs = [pl.BlockSpec((None, n_heads, head_dim, page), cache(p)) for p in range(pps)]
    grid_spec = pltpu.PrefetchScalarGridSpec(
        num_scalar_prefetch=1,
        grid=(b, n_pages // pps),
        in_specs=[pl.BlockSpec((d, b), const), pl.BlockSpec((d, b), const), pl.BlockSpec((d, b), const)]
                 + page_specs + page_specs
                 + [pl.BlockSpec((2 * page, page), const), pl.BlockSpec((n_heads * SUBLANES, 1), const)],
        out_specs=pl.BlockSpec((None, n_heads, head_dim), lambda i, j, pt: (i, 0, 0)),
        scratch_shapes=[pltpu.VMEM((n_heads, head_dim, page), F32),
                        pltpu.VMEM((n_heads, head_dim, page), F32),
                        pltpu.VMEM((n_heads * SUBLANES, 1), F32)],
    )
    return pl.pallas_call(
        functools.partial(_sb_decode_body, n_past=n_pages * page, pps=pps),
        grid_spec=grid_spec,
        out_shape=jax.ShapeDtypeStruct((b, n_heads, head_dim), F32),
        **_call_options("sb_attn_decode", 2),
    )(page_table.reshape(-1), qs_t, k_new_t, v_new_t, *([cache_k] * pps), *([cache_v] * pps), u2, bias_col)


def _s5_constants(lam_re, lam_im, log_dt, b_re, b_im, c_re, c_im, run):
    lam_re = jnp.minimum(lam_re.astype(F32), LAMBDA_RE_MAX)
    lam_im = lam_im.astype(F32)
    dt = jnp.exp(log_dt.astype(F32))[:, None]
    decay = jnp.exp(lam_re * dt)
    a_re = decay * jnp.cos(lam_im * dt)
    a_im = decay * jnp.sin(lam_im * dt)
    inv = 1.0 / (lam_re * lam_re + lam_im * lam_im)
    f_re = ((a_re - 1.0) * lam_re + a_im * lam_im) * inv
    f_im = (a_im * lam_re - (a_re - 1.0) * lam_im) * inv
    b_re, b_im = b_re.astype(F32), b_im.astype(F32)
    bb_re = f_re[..., None] * b_re - f_im[..., None] * b_im
    bb_im = f_re[..., None] * b_im + f_im[..., None] * b_re
    g, p, c = bb_re.shape
    gl = MXU_WIDTH // c
    nkb = g // gl
    eye = jnp.eye(gl, dtype=F32)

    def in_map(bb):
        m = bb.transpose(0, 2, 1).reshape(nkb, gl, c, p)
        return jnp.einsum("kgcp,gh->kgchp", m, eye).reshape(nkb, gl * c, gl * p).astype(BF16)

    def out_map(cm):
        m = cm.astype(F32).transpose(0, 2, 1).reshape(nkb, gl, p, c)
        return jnp.einsum("kgpc,gh->kgphc", m, eye).reshape(nkb, gl * p, gl * c).astype(BF16)

    ar, ai = a_re.reshape(-1), a_im.reshape(-1)

    def cmul(x, y):
        return (x[0] * y[0] - x[1] * y[1], x[0] * y[1] + x[1] * y[0])

    a1 = (ar, ai)
    pw = [a1]
    for _ in range(run - 1):
        pw.append(cmul(pw[-1], a1))
    a_pow = jnp.stack([jnp.stack([q[0] for q in pw]), jnp.stack([q[1] for q in pw])])
    hop = [pw[run - 1]]
    for _ in range(2):
        hop.append(cmul(hop[-1], hop[-1]))
    rows = jnp.arange(SUBLANES)[:, None]
    hops = jnp.stack([jnp.stack([jnp.where(rows >= (1 << k), hop[k][0][None, :], 0.0),
                                 jnp.where(rows >= (1 << k), hop[k][1][None, :], 0.0)])
                      for k in range(3)])
    a_one = jnp.stack([ar, ai]).reshape(2, 1, -1)
    return in_map(bb_re), in_map(bb_im), out_map(c_re), out_map(c_im), hops, a_pow, a_one


def _glu_out(y, wg_ref, bg_ref):
    z = _dot(y.astype(BF16), wg_ref[...]) + bg_ref[...]
    d = z.shape[1] // 2
    return z[:, :d] * jax.nn.sigmoid(z[:, d:])


def _s5_prompt_body(x_ref, g_ref, sh_ref, sc_ref, gt_ref, bbr_ref, bbi_ref, cr_ref, ci_ref, ah_ref,
                    ap_ref, a_ref, perm_ref, unperm_ref, dk_ref, wg_ref, bg_ref, o_ref, fre_ref, fim_ref,
                    y_scr, sr_scr, si_scr, st_scr, *, tl):
    i = pl.program_id(1)
    nkb, kin, cw = bbr_ref.shape
    sub = SUBLANES
    run = tl // sub
    last = sub - 1

    @pl.when(i == 0)
    def _():
        st_scr[...] = jnp.zeros_like(st_scr)

    x = x_ref[...]
    h = _modnorm(x, g_ref[...], sh_ref[...], sc_ref[...])
    perm = perm_ref[...]
    hb = _dot(perm, h.astype(BF16)).astype(BF16)
    for kb in range(nkb):
        cols = slice(kb * cw, (kb + 1) * cw)
        hk = hb[:, kb * kin:(kb + 1) * kin]
        sr_scr[...] = _dot(hk, bbr_ref[kb]).reshape(run, sub, cw)
        si_scr[...] = _dot(hk, bbi_ref[kb]).reshape(run, sub, cw)
        a_r = a_ref[0, :, cols]
        a_i = a_ref[1, :, cols]
        s_r = sr_scr[0]
        s_i = si_scr[0]
        for p in range(1, run):
            s_r, s_i = sr_scr[p] + (a_r * s_r - a_i * s_i), si_scr[p] + (a_r * s_i + a_i * s_r)
            sr_scr[p] = s_r
            si_scr[p] = s_i
        row = lax.broadcasted_iota(jnp.int32, (sub, cw), 0)
        e_r = jnp.where(row == 0, st_scr[0, :, cols], pltpu.roll(s_r, 1, 0))
        e_i = jnp.where(row == 0, st_scr[1, :, cols], pltpu.roll(s_i, 1, 0))
        for k in range(3):
            h_r = ah_ref[k, 0, :, cols]
            h_i = ah_ref[k, 1, :, cols]
            p_r = pltpu.roll(e_r, 1 << k, 0)
            p_i = pltpu.roll(e_i, 1 << k, 0)
            e_r, e_i = e_r + h_r * p_r - h_i * p_i, e_i + h_r * p_i + h_i * p_r
        n_r = ap_ref[0, run - 1:run, cols]
        n_i = ap_ref[1, run - 1:run, cols]
        st_scr[0, :, cols] = s_r[last:] + (n_r * e_r[last:] - n_i * e_i[last:])
        st_scr[1, :, cols] = s_i[last:] + (n_r * e_i[last:] + n_i * e_r[last:])
        for p in range(run):
            w_r = ap_ref[0, p:p + 1, cols]
            w_i = ap_ref[1, p:p + 1, cols]
            sr_scr[p] += w_r * e_r - w_i * e_i
            si_scr[p] += w_r * e_i + w_i * e_r
        s_re = sr_scr[...].reshape(tl, cw).astype(BF16)
        s_im = si_scr[...].reshape(tl, cw).astype(BF16)
        y_scr[:, kb * kin:(kb + 1) * kin] = _dot(s_re, cr_ref[kb]) - _dot(s_im, ci_ref[kb])
    y = y_scr[...]
    y1 = y.astype(BF16)
    y2 = (y - y1.astype(F32)).astype(BF16)
    y3 = ((y - y1.astype(F32)) - y2.astype(F32)).astype(BF16)
    unperm = unperm_ref[...]
    y = (_dot(unperm, y1) + _dot(unperm, y2)) + _dot(unperm, y3)
    y = y + dk_ref[...] * h
    o_ref[...] = x + gt_ref[...] * _glu_out(y, wg_ref, bg_ref)
    fre_ref[...] = st_scr[0]
    fim_ref[...] = st_scr[1]


def _s5_prompt_call(x, gain4, mod5, consts, d_skip, w_glu_bf, b_glu, layer, nseq, tl):
    t, d = x.shape
    ni = (t // nseq) // tl
    bbr, bbi, cr, ci, hops, a_pow, a_one = consts
    assert a_pow.shape[1] == tl // SUBLANES, "constants were built for another tile length"
    nkb, kin, cw = bbr.shape
    nch = nkb * cw
    sh, sc, gt = _mod_specs(mod5, layer, 1, ni)

    def mspec(s):
        return pl.BlockSpec(s.block_shape, lambda b, i, _f=s.index_map: _f(b * ni + i))

    def whole(a):
        nd = a.ndim
        return pl.BlockSpec(a.shape, lambda b, i: (0,) * nd, pipeline_mode=pl.Buffered(1))

    dk = d_skip.astype(F32).reshape(1, d)
    bg = b_glu.astype(F32).reshape(1, -1)
    run = tl // SUBLANES
    src = jnp.arange(tl)
    src = (src % SUBLANES) * run + src // SUBLANES
    perm = (src[:, None] == jnp.arange(tl)[None, :]).astype(BF16)
    unperm = perm.T
    tok = lambda b, i: (b * ni + i, 0)
    x_out, fre, fim = pl.pallas_call(
        functools.partial(_s5_prompt_body, tl=tl),
        grid=(nseq, ni),
        in_specs=[pl.BlockSpec((tl, d), tok),
                  pl.BlockSpec((None, None, 1, d), lambda b, i: (layer, 1, 0, 0)),
                  mspec(sh), mspec(sc), mspec(gt),
                  whole(bbr), whole(bbi), whole(cr), whole(ci), whole(hops), whole(a_pow), whole(a_one),
                  whole(perm), whole(unperm), whole(dk), whole(w_glu_bf), whole(bg)],
        out_specs=[pl.BlockSpec((tl, d), tok),
                   pl.BlockSpec((None, 1, nch), lambda b, i: (b, 0, 0)),
                   pl.BlockSpec((None, 1, nch), lambda b, i: (b, 0, 0))],
        out_shape=[jax.ShapeDtypeStruct((t, d), F32),
                   jax.ShapeDtypeStruct((nseq, 1, nch), F32),
                   jax.ShapeDtypeStruct((nseq, 1, nch), F32)],
        scratch_shapes=[pltpu.VMEM((tl, d), F32),
                        pltpu.VMEM((tl // SUBLANES, SUBLANES, cw), F32),
                        pltpu.VMEM((tl // SUBLANES, SUBLANES, cw), F32),
                        pltpu.VMEM((2, 1, nch), F32)],
        **_call_options("s5_mixer_prompt", 2),
    )(x, gain4, mod5, mod5, mod5, bbr, bbi, cr, ci, hops, a_pow, a_one, perm, unperm, dk, w_glu_bf, bg)
    return x_out, fre.reshape(nseq, nch), fim.reshape(nseq, nch)


def _s5_step_body(x_ref, g_ref, sh_ref, sc_ref, gt_ref, pre_ref, pim_ref, bbr_ref, bbi_ref, cr_ref,
                  ci_ref, a_ref, dk_ref, wg_ref, bg_ref, o_ref, nre_ref, nim_ref, y_scr):
    nkb, kin, cw = bbr_ref.shape
    x = x_ref[...]
    h = _modnorm(x, g_ref[...], sh_ref[...], sc_ref[...])
    hb = h.astype(BF16)
    for kb in range(nkb):
        cols = slice(kb * cw, (kb + 1) * cw)
        hk = hb[:, kb * kin:(kb + 1) * kin]
        ar = a_ref[0, :, cols]
        ai = a_ref[1, :, cols]
        pr = pre_ref[:, cols]
        pi = pim_ref[:, cols]
        s_r = _dot(hk, bbr_ref[kb]) + (ar * pr - ai * pi)
        s_i = _dot(hk, bbi_ref[kb]) + (ar * pi + ai * pr)
        nre_ref[:, cols] = s_r
        nim_ref[:, cols] = s_i
        y_scr[:, kb * kin:(kb + 1) * kin] = (_dot(s_r.astype(BF16), cr_ref[kb])
                                             - _dot(s_i.astype(BF16), ci_ref[kb]))
    y = y_scr[...] + dk_ref[...] * h
    o_ref[...] = x + gt_ref[...] * _glu_out(y, wg_ref, bg_ref)


def _s5_step_call(x, gain4, mod5, prev_re, prev_im, consts, d_skip, w_glu_bf, b_glu, layer):
    t, d = x.shape
    bbr, bbi, cr, ci, _, _, a_one = consts
    sh, sc, gt = _mod_specs(mod5, layer, 1, 1)

    def whole(a):
        nd = a.ndim
        return pl.BlockSpec(a.shape, lambda i: (0,) * nd)

    dk = d_skip.astype(F32).reshape(1, d)
    bg = b_glu.astype(F32).reshape(1, -1)
    return pl.pallas_call(
        _s5_step_body,
        grid=(1,),
        in_specs=[whole(x), pl.BlockSpec((None, None, 1, d), lambda i: (layer, 1, 0, 0)),
                  sh, sc, gt, whole(prev_re), whole(prev_im),
                  whole(bbr), whole(bbi), whole(cr), whole(ci), whole(a_one),
                  whole(dk), whole(w_glu_bf), whole(bg)],
        out_specs=[whole(x), whole(prev_re), whole(prev_im)],
        out_shape=[jax.ShapeDtypeStruct((t, d), F32), jax.ShapeDtypeStruct(prev_re.shape, F32),
                   jax.ShapeDtypeStruct(prev_im.shape, F32)],
        scratch_shapes=[pltpu.VMEM((t, d), F32)],
        **_call_options("s5_mixer_step", 1),
    )(x, gain4, mod5, mod5, mod5, prev_re, prev_im, bbr, bbi, cr, ci, a_one, dk, w_glu_bf, bg)


def _tile(n, want):
    t = min(n, want)
    while n % t:
        t //= 2
    return t


def kernel(x_prompt, x_sample, state_conv, cache_k, cache_v, state_ssm_re, state_ssm_im, page_table, c_prompt, c_sample, ln_gain, ada_w, ada_b, ffn_w13, ffn_w2, conv_w_in, conv_w, conv_w_out, attn_w_qkv, attn_q_gain, attn_k_gain, attn_logit_bias, attn_w_o, ssm_lambda_re, ssm_lambda_im, ssm_log_dt, ssm_b_re, ssm_b_im, ssm_c_re, ssm_c_im, ssm_d, ssm_w_glu, ssm_b_glu):
    bp, seq, d = x_prompt.shape
    bs, seq_s, _ = x_sample.shape
    assert seq_s == 1, "the sample trunk handles one new token per sequence"
    depth = ln_gain.shape[0]
    n_heads, head_dim = cache_k.shape[3], cache_k.shape[4]
    page = cache_k.shape[2]
    width = conv_w.shape[1]
    n_state = ssm_lambda_re.shape[1] * ssm_lambda_re.shape[2]

    rows_p = -(-bp // SUBLANES) * SUBLANES
    c_all = jnp.concatenate([c_prompt, jnp.zeros((rows_p - bp, d), F32), c_sample], axis=0)
    mod_p, mod_s = _ada_call(c_all, ada_w, ada_b, rows_p)
    mod_p = mod_p.reshape(depth, N_SUB * 3, rows_p, 1, d)
    mod_s = mod_s.reshape(depth, N_SUB * 3, 1, bs, d)
    gain4 = ln_gain.reshape(depth, N_SUB, 1, d)

    xp = x_prompt.reshape(bp * seq, d)
    xs = x_sample.reshape(bs, d)
    tm_p = _tile(seq, TOKEN_TILE)
    tps = seq // tm_p

    outs = dict(pc=[], pk=[], pv=[], pr=[], pi=[], sc=[], sk=[], sv=[], sr=[], si=[])
    for i in range(depth):
        kind, j = i % N_MIXERS, i // N_MIXERS
        xs, *w_bf = _ffn_cast_call(xs, gain4, mod_s, ffn_w13, ffn_w2, i, 0, 0)
        xp = _ffn_call(xp, gain4, mod_p, *w_bf, i, 0, tm_p, tps)
        if kind == 0:
            w_in = conv_w_in[j].astype(BF16)
            w_out = conv_w_out[j].astype(BF16)
            prev_p = jnp.zeros((bp, width - 1, d), F32)
            xp, st = _conv_prompt_call(xp, gain4, mod_p, prev_p, w_in, conv_w[j], w_out, i, bp, tm_p)
            outs["pc"].append(st)
            xs, st = _conv_step_call(xs, gain4, mod_s, state_conv[j].reshape(bs, (width - 1) * d),
                                     w_in, conv_w[j], w_out, i)
            outs["sc"].append(st.reshape(bs, width - 1, d))
        elif kind == 1:
            w_qkv = attn_w_qkv[j].astype(BF16)
            w_o = attn_w_o[j].astype(BF16)
            qs, kb, vb, k_t, v_t = _qkv_call(xp, gain4, mod_p, w_qkv, attn_q_gain[j], attn_k_gain[j], i,
                                             bp, tm_p, head_dim, decode=False)
            outs["pk"].append(k_t.reshape(bp, n_heads, head_dim, seq).transpose(0, 3, 1, 2))
            outs["pv"].append(v_t.reshape(bp, n_heads, head_dim, seq).transpose(0, 3, 1, 2))
            o = _sb_prompt_call(qs, kb, vb, attn_logit_bias[j], bp, head_dim, _tile(seq, ATTN_Q_TILE),
                                _tile(seq, ATTN_K_TILE))
            xp = _proj_res_call(xp, o, mod_p, w_o, i, tm_p, tps)
            qs_t, k_t, v_t = _qkv_call(xs, gain4, mod_s, w_qkv, attn_q_gain[j], attn_k_gain[j], i,
                                       1, bs, head_dim, decode=True)
            k_t, v_t = k_t[0], v_t[0]
            outs["sk"].append(k_t.reshape(n_heads, head_dim, bs, 1).transpose(2, 3, 0, 1))
            outs["sv"].append(v_t.reshape(n_heads, head_dim, bs, 1).transpose(2, 3, 0, 1))
            o = _sb_decode_call(qs_t, k_t, v_t, cache_k[j].transpose(0, 2, 3, 1),
                                cache_v[j].transpose(0, 2, 3, 1), page_table, attn_logit_bias[j])
            xs = _proj_res_call(xs, o.reshape(bs, d), mod_s, w_o, i, bs, 1)
        else:
            tl = _tile(seq, S5_TILE)
            consts = _s5_constants(ssm_lambda_re[j], ssm_lambda_im[j], ssm_log_dt[j], ssm_b_re[j],
                                   ssm_b_im[j], ssm_c_re[j], ssm_c_im[j], tl // SUBLANES)
            w_glu = ssm_w_glu[j].astype(BF16)
            xp, fre, fim = _s5_prompt_call(xp, gain4, mod_p, consts, ssm_d[j], w_glu, ssm_b_glu[j], i,
                                           bp, tl)
            outs["pr"].append(fre.reshape(bp, -1, ssm_lambda_re.shape[2]))
            outs["pi"].append(fim.reshape(bp, -1, ssm_lambda_re.shape[2]))
            xs, nre, nim = _s5_step_call(xs, gain4, mod_s, state_ssm_re[j].reshape(bs, n_state),
                                         state_ssm_im[j].reshape(bs, n_state), consts, ssm_d[j], w_glu,
                                         ssm_b_glu[j], i)
            outs["sr"].append(nre.reshape(state_ssm_re.shape[1:]))
            outs["si"].append(nim.reshape(state_ssm_im.shape[1:]))
        xs, *w_bf = _ffn_cast_call(xs, gain4, mod_s, ffn_w13, ffn_w2, i, 1, 2)
        xp = _ffn_call(xp, gain4, mod_p, *w_bf, i, 2, tm_p, tps)

    st = {k: jnp.stack(v) for k, v in outs.items()}
    return (xp.reshape(bp, seq, d), xs.reshape(bs, 1, d), st["pc"], st["pk"], st["pv"], st["pr"],
            st["pi"], st["sc"], st["sk"], st["sv"], st["sr"], st["si"])
```

```python
import functools

import jax
import jax.numpy as jnp
from jax import lax
from jax.experimental import pallas as pl
from jax.experimental.pallas import tpu as pltpu

F32 = jnp.float32
BF16 = jnp.bfloat16

N_MIXERS = 3
N_SUB = 3
RMS_EPS = 1e-6
FFN_RES_WEIGHT = 0.5
LAMBDA_RE_MAX = -1e-4
MASKED_LOG = -1e30
SUBLANES = 8
LANES = 128
MXU_WIDTH = 256
MIB = 1024 * 1024

TOKEN_TILE = 512
S5_TILE = 256
ATTN_Q_TILE = 1024
ATTN_K_TILE = MXU_WIDTH
DECODE_PAGES_PER_STEP = 16
VMEM_MIB = {"ada_mod": 32, "ffn_swiglu": 56, "ffn_swiglu_cast": 32, "proj_residual": 32,
            "conv_mixer_prompt": 40, "conv_mixer_step": 40, "qkv_proj": 40, "sb_attn_prompt": 40,
            "sb_attn_decode": 48, "s5_mixer_prompt": 48, "s5_mixer_step": 48}


def _call_options(name, grid_rank):
    return dict(name=name,
                compiler_params=pltpu.CompilerParams(dimension_semantics=("arbitrary",) * grid_rank,
                                                     vmem_limit_bytes=VMEM_MIB[name] * MIB))


def _dot(a, b):
    return jnp.dot(a, b, preferred_element_type=F32)


def _dot_nt(a, b):
    return lax.dot_general(a, b, (((1,), (1,)), ((), ())), preferred_element_type=F32)


def _hi_lo(x):
    hi = x.astype(BF16)
    lo = (x - hi.astype(F32)).astype(BF16)
    return hi, lo


def _modnorm(x, gain, shift, scale):
    ms = jnp.mean(x * x, axis=-1, keepdims=True)
    y = x * lax.rsqrt(ms + RMS_EPS)
    return (y * gain) * (1.0 + scale) + shift


def _silu(x):
    return x * jax.nn.sigmoid(x)


def _mod_specs(mod5, layer, sub, tiles_per_seq):
    r, d = mod5.shape[3], mod5.shape[4]

    def spec(t):
        return pl.BlockSpec((None, None, None, r, d),
                            lambda i, *_: (layer, N_SUB * sub + t, i // tiles_per_seq, 0, 0))

    return spec(0), spec(1), spec(2)


def _ada_body(c_ref, w_ref, b_ref, op_ref, os_ref, *, rows_p):
    ca = _silu(c_ref[...]).astype(BF16)
    m = _dot(ca, w_ref[...].astype(BF16)) + b_ref[...]
    op_ref[...] = m[:rows_p]
    os_ref[...] = m[rows_p:]


def _ada_call(c_all, ada_w, ada_b, rows_p):
    depth, d, n = ada_w.shape
    nrow = n // d
    rows = c_all.shape[0]
    rows_s = rows - rows_p
    b4 = ada_b.reshape(depth, nrow, 1, d)
    return pl.pallas_call(
        functools.partial(_ada_body, rows_p=rows_p),
        grid=(depth, nrow),
        in_specs=[pl.BlockSpec((rows, d), lambda l, j: (0, 0)),
                  pl.BlockSpec((None, d, d), lambda l, j: (l, 0, j)),
                  pl.BlockSpec((None, None, 1, d), lambda l, j: (l, j, 0, 0))],
        out_specs=[pl.BlockSpec((None, None, rows_p, d), lambda l, j: (l, j, 0, 0)),
                   pl.BlockSpec((None, None, rows_s, d), lambda l, j: (l, j, 0, 0))],
        out_shape=[jax.ShapeDtypeStruct((depth, nrow, rows_p, d), F32),
                   jax.ShapeDtypeStruct((depth, nrow, rows_s, d), F32)],
        **_call_options("ada_mod", 2),
    )(c_all, ada_w, b4)


def _ffn_body(x_ref, g_ref, sh_ref, sc_ref, gt_ref, w1_ref, w3_ref, w2_ref, *rest, tf):
    *mixer, o_ref = rest
    dff = w2_ref.shape[0]
    x = x_ref[...]
    if mixer:
        a_ref, gm_ref, wo_ref = mixer
        x = x + gm_ref[...] * _dot(a_ref[...], wo_ref[...])
    h = _modnorm(x, g_ref[...], sh_ref[...], sc_ref[...]).astype(BF16)
    parts = []
    for c in range(0, dff, tf):
        g = _dot(h, w1_ref[:, c:c + tf])
        u = _dot(h, w3_ref[:, c:c + tf])
        parts.append((_silu(g) * u).astype(BF16))
    a = jnp.concatenate(parts, axis=1)
    o_ref[...] = x + (FFN_RES_WEIGHT * gt_ref[...]) * _dot(a, w2_ref[...])


def _ffn_call(x, gain4, mod5, w1_bf, w3_bf, w2_bf, layer, sub, tm, tiles_per_seq, mixer_proj=None,
              tf=MXU_WIDTH):
    t, d = x.shape
    sh, sc, gt = _mod_specs(mod5, layer, sub, tiles_per_seq)
    once = pl.Buffered(1)
    const = lambda i: (0, 0)
    extra_specs, extra_args = [], []
    if mixer_proj is not None:
        a, w_o_bf = mixer_proj
        extra_specs = [pl.BlockSpec((tm, a.shape[1]), lambda i: (i, 0)),
                       _mod_specs(mod5, layer, 1, tiles_per_seq)[2],
                       pl.BlockSpec(w_o_bf.shape, const, pipeline_mode=once)]
        extra_args = [a, mod5, w_o_bf]
    return pl.pallas_call(
        functools.partial(_ffn_body, tf=tf),
        grid=(t // tm,),
        in_specs=[pl.BlockSpec((tm, d), lambda i: (i, 0)),
                  pl.BlockSpec((None, None, 1, d), lambda i: (layer, sub, 0, 0)),
                  sh, sc, gt,
                  pl.BlockSpec(w1_bf.shape, const, pipeline_mode=once),
                  pl.BlockSpec(w3_bf.shape, const, pipeline_mode=once),
                  pl.BlockSpec(w2_bf.shape, const, pipeline_mode=once)] + extra_specs,
        out_specs=pl.BlockSpec((tm, d), lambda i: (i, 0)),
        out_shape=jax.ShapeDtypeStruct((t, d), F32),
        **_call_options("ffn_swiglu", 1),
    )(x, gain4, mod5, mod5, mod5, w1_bf, w3_bf, w2_bf, *extra_args)


def _ffn_cast_body(x_ref, g_ref, sh_ref, sc_ref, gt_ref, w1_ref, w3_ref, w2_ref,
                   o_ref, w1b_ref, w3b_ref, w2b_ref, h_scr, acc_scr):
    j = pl.program_id(0)

    @pl.when(j == 0)
    def _():
        h_scr[...] = _modnorm(x_ref[...], g_ref[...], sh_ref[...], sc_ref[...]).astype(BF16)
        acc_scr[...] = jnp.zeros_like(acc_scr)

    w1 = w1_ref[...].astype(BF16)
    w3 = w3_ref[...].astype(BF16)
    w2 = w2_ref[...].astype(BF16)
    w1b_ref[...] = w1
    w3b_ref[...] = w3
    w2b_ref[...] = w2
    h = h_scr[...]
    a = (_silu(_dot(h, w1)) * _dot(h, w3)).astype(BF16)
    acc_scr[...] += _dot(a, w2)

    @pl.when(j == pl.num_programs(0) - 1)
    def _():
        o_ref[...] = x_ref[...] + (FFN_RES_WEIGHT * gt_ref[...]) * acc_scr[...]


def _ffn_cast_call(x, gain4, mod5, w13, w2, layer, which, sub, tf=MXU_WIDTH):
    t, d = x.shape
    dff = w2.shape[2]
    nf = dff // tf
    sh, sc, gt = _mod_specs(mod5, layer, sub, 1)
    fix = lambda spec: pl.BlockSpec(spec.block_shape, lambda j, _f=spec.index_map: _f(0))
    whole = lambda j: (0, 0)
    return pl.pallas_call(
        _ffn_cast_body,
        grid=(nf,),
        in_specs=[pl.BlockSpec((t, d), whole),
                  pl.BlockSpec((None, None, 1, d), lambda j: (layer, sub, 0, 0)),
                  fix(sh), fix(sc), fix(gt),
                  pl.BlockSpec((None, None, d, tf), lambda j: (layer, which, 0, j)),
                  pl.BlockSpec((None, None, d, tf), lambda j: (layer, which, 0, j + nf)),
                  pl.BlockSpec((None, None, tf, d), lambda j: (layer, which, j, 0))],
        out_specs=[pl.BlockSpec((t, d), whole),
                   pl.BlockSpec((d, tf), lambda j: (0, j)),
                   pl.BlockSpec((d, tf), lambda j: (0, j)),
                   pl.BlockSpec((tf, d), lambda j: (j, 0))],
        out_shape=[jax.ShapeDtypeStruct((t, d), F32),
                   jax.ShapeDtypeStruct((d, dff), BF16), jax.ShapeDtypeStruct((d, dff), BF16),
                   jax.ShapeDtypeStruct((dff, d), BF16)],
        scratch_shapes=[pltpu.VMEM((t, d), BF16), pltpu.VMEM((t, d), F32)],
        **_call_options("ffn_swiglu_cast", 1),
    )(x, gain4, mod5, mod5, mod5, w13, w13, w2)


def _proj_res_body(x_ref, a_ref, gt_ref, w_ref, o_ref):
    o_ref[...] = x_ref[...] + gt_ref[...] * _dot(a_ref[...].astype(BF16), w_ref[...])


def _proj_res_call(x, a, mod5, w_bf, layer, tm, tiles_per_seq):
    t, d = x.shape
    _, _, gt = _mod_specs(mod5, layer, 1, tiles_per_seq)
    return pl.pallas_call(
        _proj_res_body,
        grid=(t // tm,),
        in_specs=[pl.BlockSpec((tm, d), lambda i: (i, 0)),
                  pl.BlockSpec((tm, a.shape[1]), lambda i: (i, 0)),
                  gt,
                  pl.BlockSpec(w_bf.shape, lambda i: (0, 0))],
        out_specs=pl.BlockSpec((tm, d), lambda i: (i, 0)),
        out_shape=jax.ShapeDtypeStruct((t, d), F32),
        **_call_options("proj_residual", 1),
    )(x, a, mod5, w_bf)


def _conv_prompt_body(x_ref, g_ref, sh_ref, sc_ref, gt_ref, prev_ref, win_ref, cw_ref, wo_ref,
                      o_ref, st_ref, u_scr, *, tm, tc, width):
    i = pl.program_id(1)
    d = x_ref.shape[1]
    halo = SUBLANES
    first = halo - (width - 1)

    @pl.when(i == 0)
    def _():
        u_scr[first:halo, :] = prev_ref[...]

    x = x_ref[...]
    h = _modnorm(x, g_ref[...], sh_ref[...], sc_ref[...]).astype(BF16)
    parts = []
    for c in range(0, d, tc):
        cols = slice(c, c + tc)
        bg = _dot(h, win_ref[:, cols])
        cg = _dot(h, win_ref[:, d + c:d + c + tc])
        xi = _dot(h, win_ref[:, 2 * d + c:2 * d + c + tc])
        u_scr[halo:halo + tm, cols] = cg * xi
        conv = u_scr[first:first + tm, cols] * cw_ref[0:1, cols]
        for tap in range(1, width):
            conv = conv + u_scr[first + tap:first + tap + tm, cols] * cw_ref[tap:tap + 1, cols]
        parts.append((bg * conv).astype(BF16))
    o_ref[...] = x + gt_ref[...] * _dot(jnp.concatenate(parts, axis=1), wo_ref[...])
    tail = u_scr[tm:tm + halo, :]
    u_scr[0:halo, :] = tail
    st_ref[...] = tail[first:]


def _conv_prompt_call(x, gain4, mod5, prev, w_in_bf, conv_w, w_out_bf, layer, nseq, tm, tc=MXU_WIDTH):
    t, d = x.shape
    seq = t // nseq
    ni = seq // tm
    width = conv_w.shape[0]
    sh, sc, gt = _mod_specs(mod5, layer, 1, ni)
    once = pl.Buffered(1)
    tok = lambda b, i: (b * ni + i, 0)
    const = lambda b, i: (0, 0)

    def mspec(s):
        return pl.BlockSpec(s.block_shape, lambda b, i, _f=s.index_map: _f(b * ni + i))

    return pl.pallas_call(
        functools.partial(_conv_prompt_body, tm=tm, tc=tc, width=width),
        grid=(nseq, ni),
        in_specs=[pl.BlockSpec((tm, d), tok),
                  pl.BlockSpec((None, None, 1, d), lambda b, i: (layer, 1, 0, 0)),
                  mspec(sh), mspec(sc), mspec(gt),
                  pl.BlockSpec((None, width - 1, d), lambda b, i: (b, 0, 0)),
                  pl.BlockSpec(w_in_bf.shape, const, pipeline_mode=once),
                  pl.BlockSpec(conv_w.shape, const),
                  pl.BlockSpec(w_out_bf.shape, const, pipeline_mode=once)],
        out_specs=[pl.BlockSpec((tm, d), tok),
                   pl.BlockSpec((None, width - 1, d), lambda b, i: (b, 0, 0))],
        out_shape=[jax.ShapeDtypeStruct((t, d), F32),
                   jax.ShapeDtypeStruct((nseq, width - 1, d), F32)],
        scratch_shapes=[pltpu.VMEM((tm + SUBLANES, d), F32)],
        **_call_options("conv_mixer_prompt", 2),
    )(x, gain4, mod5, mod5, mod5, prev, w_in_bf, conv_w, w_out_bf)


def _conv_step_body(x_ref, g_ref, sh_ref, sc_ref, gt_ref, prev_ref, wb_ref, wc_ref, wx_ref, cw_ref,
                    wo_ref, o_ref, st_ref, *, width):
    d = x_ref.shape[1]
    x = x_ref[...]
    h = _modnorm(x, g_ref[...], sh_ref[...], sc_ref[...]).astype(BF16)
    bg = _dot(h, wb_ref[...])
    u = _dot(h, wc_ref[...]) * _dot(h, wx_ref[...])
    cw = cw_ref[...]
    taps = [prev_ref[:, k * d:(k + 1) * d] for k in range(width - 1)] + [u]
    conv = taps[0] * cw[0:1]
    for k in range(1, width):
        conv = conv + taps[k] * cw[k:k + 1]
    o_ref[...] = x + gt_ref[...] * _dot((bg * conv).astype(BF16), wo_ref[...])
    for k in range(width - 1):
        st_ref[:, k * d:(k + 1) * d] = taps[k + 1]


def _conv_step_call(x, gain4, mod5, prev2, w_in_bf, conv_w, w_out_bf, layer):
    t, d = x.shape
    width = conv_w.shape[0]
    sh, sc, gt = _mod_specs(mod5, layer, 1, 1)
    full = lambda i: (0, 0)
    return pl.pallas_call(
        functools.partial(_conv_step_body, width=width),
        grid=(1,),
        in_specs=[pl.BlockSpec((t, d), full),
                  pl.BlockSpec((None, None, 1, d), lambda i: (layer, 1, 0, 0)),
                  sh, sc, gt,
                  pl.BlockSpec(prev2.shape, full),
                  pl.BlockSpec((d, d), lambda i: (0, 0)),
                  pl.BlockSpec((d, d), lambda i: (0, 1)),
                  pl.BlockSpec((d, d), lambda i: (0, 2)),
                  pl.BlockSpec(conv_w.shape, full),
                  pl.BlockSpec((d, d), full)],
        out_specs=[pl.BlockSpec((t, d), full), pl.BlockSpec(prev2.shape, full)],
        out_shape=[jax.ShapeDtypeStruct((t, d), F32), jax.ShapeDtypeStruct(prev2.shape, F32)],
        **_call_options("conv_mixer_step", 1),
    )(x, gain4, mod5, mod5, mod5, prev2, w_in_bf, w_in_bf, w_in_bf, conv_w, w_out_bf)


def _qkv_body(x_ref, g_ref, sh_ref, sc_ref, w_ref, qg_ref, kg_ref, seg_ref, *rest,
              head_dim, q_scale, tn, decode):
    d = x_ref.shape[1]
    h = _modnorm(x_ref[...], g_ref[...], sh_ref[...], sc_ref[...]).astype(BF16)
    seg = seg_ref[...]

    def head_norm(y, gain):
        hi, lo = _hi_lo(y * y)
        ms = (_dot(hi, seg) + _dot(lo, seg)) * (1.0 / head_dim)
        return (y * lax.rsqrt(ms + RMS_EPS)) * gain

    for c in range(0, d, tn):
        cols = slice(c, c + tn)
        q = head_norm(_dot(h, w_ref[:, cols]), qg_ref[...]) * q_scale
        k = head_norm(_dot(h, w_ref[:, d + c:d + c + tn]), kg_ref[...])
        v = _dot(h, w_ref[:, 2 * d + c:2 * d + c + tn])
        if decode:
            qs_ref, kt_ref, vt_ref, t_scr = rest
            qs_ref[cols, :] = q.T.astype(BF16)
        else:
            qs_ref, kb_ref, vb_ref, kt_ref, vt_ref, t_scr = rest
            qs_ref[:, cols] = q.astype(BF16)
            kb_ref[:, cols] = k.astype(BF16)
            vb_ref[:, cols] = v.astype(BF16)
        kt_ref[cols, :] = k.T
        t_scr[...] = v
        vt_ref[cols, :] = t_scr[...].T


def _qkv_call(x, gain4, mod5, w_qkv_bf, q_gain, k_gain, layer, nseq, tm, head_dim, decode, tn=MXU_WIDTH):
    t, d = x.shape
    seq = t // nseq
    ni = seq // tm
    sh, sc, _ = _mod_specs(mod5, layer, 1, ni)
    reps = tn // head_dim
    qg = jnp.tile(q_gain.astype(F32), reps).reshape(1, tn)
    kg = jnp.tile(k_gain.astype(F32), reps).reshape(1, tn)
    lane_head = jnp.arange(tn) // head_dim
    seg = (lane_head[:, None] == lane_head[None, :]).astype(BF16)
    const = lambda i: (0, 0)
    tok = pl.BlockSpec((tm, d), lambda i: (i, 0))
    feat = pl.BlockSpec((None, d, tm), lambda i: (i // ni, 0, i % ni))
    feat_shape = jax.ShapeDtypeStruct((nseq, d, seq), F32)
    if decode:
        out_specs = [pl.BlockSpec((d, tm), lambda i: (0, i)), feat, feat]
        out_shape = [jax.ShapeDtypeStruct((d, t), BF16), feat_shape, feat_shape]
    else:
        out_specs = [tok, tok, tok, feat, feat]
        out_shape = [jax.ShapeDtypeStruct((t, d), BF16)] * 3 + [feat_shape, feat_shape]
    return pl.pallas_call(
        functools.partial(_qkv_body, head_dim=head_dim, q_scale=head_dim ** -0.5, tn=tn, decode=decode),
        grid=(t // tm,),
        in_specs=[tok,
                  pl.BlockSpec((None, None, 1, d), lambda i: (layer, 1, 0, 0)),
                  sh, sc,
                  pl.BlockSpec(w_qkv_bf.shape, const, pipeline_mode=pl.Buffered(1)),
                  pl.BlockSpec((1, tn), const), pl.BlockSpec((1, tn), const),
                  pl.BlockSpec((tn, tn), const)],
        out_specs=out_specs,
        out_shape=out_shape,
        scratch_shapes=[pltpu.VMEM((tm, tn), F32)],
        **_call_options("qkv_proj", 1),
    )(x, gain4, mod5, mod5, w_qkv_bf, qg, kg, seg)


def _log1p_exp_neg_abs(z):
    sign_bit = jnp.uint32(0x80000000)
    neg_abs = lax.bitcast_convert_type(lax.bitcast_convert_type(z, jnp.uint32) | sign_bit, F32)
    return jnp.log(1.0 + jnp.exp(neg_abs))


def _softplus(z):
    return jnp.maximum(z, 0.0) + _log1p_exp_neg_abs(z)


def _log_sigmoid_pair(z):
    lb = jnp.minimum(z, 0.0) - _log1p_exp_neg_abs(z)
    return lb, lb - z


def _sb_prompt_body(bias_ref, q_ref, k_ref, v_ref, u_ref, o_ref, q2_scr, o_scr, c_scr, z_scr,
                    hi_scr, lo_scr, *, tq, tk, head_dim):
    hp = pl.program_id(1)
    qi = pl.program_id(2)
    n_diag = tq // tk
    n_chunks = (qi + 1) * n_diag
    q = q_ref[...]
    lane = lax.broadcasted_iota(jnp.int32, (tk, q.shape[1]), 1)
    for blk in range(n_diag):
        q_blk = q[blk * tk:(blk + 1) * tk]
        q2_scr[2 * blk * tk:(2 * blk + 1) * tk] = jnp.where(lane < head_dim, q_blk, jnp.zeros_like(q_blk))
        q2_scr[(2 * blk + 1) * tk:(2 * blk + 2) * tk] = jnp.where(lane >= head_dim, q_blk,
                                                                 jnp.zeros_like(q_blk))
    row2 = lax.broadcasted_iota(jnp.int32, (2 * tq, 1), 0)
    second = (row2 // tk) % 2 == 1
    bias = jnp.where(second, bias_ref[2 * hp + 1], bias_ref[2 * hp])
    q_row = (row2 // (2 * tk)) * tk + row2 % tk
    u = u_ref[...]
    o_scr[...] = jnp.zeros_like(o_scr)
    c_scr[...] = jnp.zeros_like(c_scr)

    def chunk_rows(m):
        return pl.ds(pl.multiple_of((n_chunks - 1 - m) * tk, tk), tk)

    def live_rows(diag):
        return slice(0 if diag is None else 2 * diag * tk, 2 * tq)

    def stage1(m, slot, diag):
        rows = live_rows(diag)
        z = _dot_nt(q2_scr[rows], k_ref[chunk_rows(m), :]) + bias[rows]
        n1 = _softplus(z)

        def keep(part, z_part, n1_part):
            hi, lo = _hi_lo(n1_part)
            z_scr[slot, part] = z_part
            hi_scr[slot, part] = hi
            lo_scr[slot, part] = lo

        if diag is None:
            keep(rows, z, n1)
        else:
            edge = 2 * tk
            first = slice(rows.start, rows.start + edge)
            col = lax.broadcasted_iota(jnp.int32, (edge, tk), 1)
            mask = (col + diag * tk) < q_row[first]
            keep(first, jnp.where(mask, z[:edge], MASKED_LOG), jnp.where(mask, n1[:edge], 0.0))
            if rows.start + edge < rows.stop:
                keep(slice(rows.start + edge, rows.stop), z[edge:], n1[edge:])

    def stage2(m, slot, diag):
        rows = live_rows(diag)
        suffix = _dot(hi_scr[slot, rows], u) + _dot(lo_scr[slot, rows], u)
        c = c_scr[rows]
        w = jnp.exp(z_scr[slot, rows] + (suffix + c))
        c_scr[rows] = c + suffix[:, 0:1]
        o_scr[rows] += _dot(w.astype(BF16), v_ref[chunk_rows(m), :])

    for m in range(n_diag):
        stage1(m, m % 2, n_diag - 1 - m)
        if m:
            stage2(m - 1, (m - 1) % 2, n_diag - m)

    def run(first, trips, unroll):
        def body(jj, carry):
            for k in range(unroll):
                m = first + unroll * jj + k
                stage1(m, (n_diag + k) % 2, None)
                stage2(m - 1, (n_diag + k - 1) % 2, None)
            return carry

        lax.fori_loop(0, trips, body, 0)

    rest = n_chunks - n_diag
    run(n_diag, rest // 4, 4)
    run(n_diag + (rest // 4) * 4, (rest % 4) // 2, 2)
    stage2(n_chunks - 1, (n_diag - 1) % 2, None)
    for blk in range(n_diag):
        o_ref[blk * tk:(blk + 1) * tk] = jnp.where(lane < head_dim, o_scr[2 * blk * tk:(2 * blk + 1) * tk],
                                                   o_scr[(2 * blk + 1) * tk:(2 * blk + 2) * tk]).astype(BF16)


def _sb_prompt_call(qs, kb, vb, bias, nseq, head_dim, tq, tk):
    t, d = qs.shape
    seq = t // nseq
    nq = seq // tq
    pair = 2 * head_dim
    assert (tq // tk) % 2 == 0, "the chunk loop is unrolled by two"
    r = jnp.arange(tk)
    u = -(r[:, None] >= r[None, :]).astype(BF16)
    return pl.pallas_call(
        functools.partial(_sb_prompt_body, tq=tq, tk=tk, head_dim=head_dim),
        grid=(nseq, d // pair, nq),
        in_specs=[pl.BlockSpec(memory_space=pltpu.SMEM),
                  pl.BlockSpec((tq, pair), lambda b, p, i: (b * nq + i, p)),
                  pl.BlockSpec((seq, pair), lambda b, p, i: (b, p)),
                  pl.BlockSpec((seq, pair), lambda b, p, i: (b, p)),
                  pl.BlockSpec((tk, tk), lambda b, p, i: (0, 0))],
        out_specs=pl.BlockSpec((tq, pair), lambda b, p, i: (b * nq + i, p)),
        out_shape=jax.ShapeDtypeStruct((t, d), BF16),
        scratch_shapes=[pltpu.VMEM((2 * tq, pair), BF16), pltpu.VMEM((2 * tq, pair), F32),
                        pltpu.VMEM((2 * tq, 1), F32),
                        pltpu.VMEM((2, 2 * tq, tk), F32), pltpu.VMEM((2, 2 * tq, tk), BF16),
                        pltpu.VMEM((2, 2 * tq, tk), BF16)],
        **_call_options("sb_attn_prompt", 3),
    )(bias.astype(F32), qs, kb, vb, u)


def _sb_decode_body(pt_ref, q_ref, kn_ref, vn_ref, *rest, n_past, pps):
    kc_refs, vc_refs = rest[:pps], rest[pps:2 * pps]
    u2_ref, bias_ref, o_ref, q_scr, acc_scr, c_scr = rest[2 * pps:]
    seq = pl.program_id(0)
    j = pl.program_id(1)
    nh, hd, page = kc_refs[0].shape
    sub = SUBLANES
    bias = bias_ref[...]

    def bf(x):
        return x.astype(BF16).astype(F32)

    def column(ref):
        lane = lax.broadcasted_iota(jnp.int32, ref.shape, 1)
        col = jnp.sum(jnp.where(lane == seq, bf(ref[...]), 0.0), axis=1, keepdims=True)
        return jnp.broadcast_to(col, (nh * hd, page)).reshape(nh, hd, page)

    def logits(keys_of_head):
        rows = []
        for h in range(nh):
            part = (bf(keys_of_head(h)) * q_scr[h]).reshape(hd // sub, sub, page).sum(axis=0)
            for s in (4, 2, 1):
                part = part + pltpu.roll(part, s, 0)
            rows.append(part)
        return jnp.concatenate(rows, axis=0) + bias

    def accumulate(w, vals_of_head):
        for h in range(nh):
            wh = w[sub * h:sub * (h + 1)]
            vals = bf(vals_of_head(h)).reshape(hd // sub, sub, page)
            acc_scr[h] += (vals * wh[None]).reshape(hd, page)

    @pl.when(j == 0)
    def _():
        q_scr[...] = column(q_ref)
        kn = column(kn_ref)
        vn = column(vn_ref)
        q_pos = n_past
        k_pos = n_past
        lb, _ = _log_sigmoid_pair(logits(lambda h: kn[h]))
        w_new = bf(jnp.where(k_pos < q_pos, jnp.exp(lb), 0.0)) * (1.0 / page)
        acc_scr[...] = jnp.zeros_like(acc_scr)
        accumulate(w_new, lambda h: vn[h])
        c_scr[...] = jnp.zeros_like(c_scr)

    u2 = u2_ref[...]
    for p in range(pps):
        z = logits(lambda h: kc_refs[p][h])
        lb, l1 = _log_sigmoid_pair(z)
        hi, lo = _hi_lo(l1)
        suffix = _dot(jnp.concatenate([hi, lo], axis=1), u2)
        c = c_scr[...]
        w = bf(jnp.exp(lb + (suffix + c)))
        c_scr[...] = c + (suffix[:, 0:1] + l1[:, 0:1])
        accumulate(w, lambda h: vc_refs[p][h])

    @pl.when(j == pl.num_programs(1) - 1)
    def _():
        o_ref[...] = jnp.sum(acc_scr[...], axis=-1)


def _sb_decode_call(qs_t, k_new_t, v_new_t, cache_k, cache_v, page_table, bias,
                    pages_per_step=DECODE_PAGES_PER_STEP):
    d, b = qs_t.shape
    n_pages = page_table.shape[1]
    _, n_heads, head_dim, page = cache_k.shape
    pps = _tile(n_pages, pages_per_step)
    r = jnp.arange(page)
    u = (r[:, None] > r[None, :]).astype(BF16)
    u2 = jnp.concatenate([u, u], axis=0)
    bias_col = jnp.repeat(bias.astype(F32), SUBLANES).reshape(n_heads * SUBLANES, 1)

    def cache(p):
        return lambda i, j, pt: (pt[i * n_pages + (n_pages - 1 - (j * pps + p))], 0, 0, 0)

    const = lambda i, j, pt: (0, 0)
    page_specs = [pl.BlockSpec((None, n_heads, head_dim, page), cache(p)) for p in range(pps)]
    grid_spec = pltpu.PrefetchScalarGridSpec(
        num_scalar_prefetch=1,
        grid=(b, n_pages // pps),
        in_specs=[pl.BlockSpec((d, b), const), pl.BlockSpec((d, b), const), pl.BlockSpec((d, b), const)]
                 + page_specs + page_specs
                 + [pl.BlockSpec((2 * page, page), const), pl.BlockSpec((n_heads * SUBLANES, 1), const)],
        out_specs=pl.BlockSpec((None, n_heads, head_dim), lambda i, j, pt: (i, 0, 0)),
        scratch_shapes=[pltpu.VMEM((n_heads, head_dim, page), F32),
                        pltpu.VMEM((n_heads, head_dim, page), F32),
                        pltpu.VMEM((n_heads * SUBLANES, 1), F32)],
    )
    return pl.pallas_call(
        functools.partial(_sb_decode_body, n_past=n_pages * page, pps=pps),
        grid_spec=grid_spec,
        out_shape=jax.ShapeDtypeStruct((b, n_heads, head_dim), F32),
        **_call_options("sb_attn_decode", 2),
    )(page_table.reshape(-1), qs_t, k_new_t, v_new_t, *([cache_k] * pps), *([cache_v] * pps), u2, bias_col)


def _s5_constants(lam_re, lam_im, log_dt, b_re, b_im, c_re, c_im, run):
    lam_re = jnp.minimum(lam_re.astype(F32), LAMBDA_RE_MAX)
    lam_im = lam_im.astype(F32)
    dt = jnp.exp(log_dt.astype(F32))[:, None]
    decay = jnp.exp(lam_re * dt)
    a_re = decay * jnp.cos(lam_im * dt)
    a_im = decay * jnp.sin(lam_im * dt)
    inv = 1.0 / (lam_re * lam_re + lam_im * lam_im)
    f_re = ((a_re - 1.0) * lam_re + a_im * lam_im) * inv
    f_im = (a_im * lam_re - (a_re - 1.0) * lam_im) * inv
    b_re, b_im = b_re.astype(F32), b_im.astype(F32)
    bb_re = f_re[..., None] * b_re - f_im[..., None] * b_im
    bb_im = f_re[..., None] * b_im + f_im[..., None] * b_re
    g, p, c = bb_re.shape
    gl = MXU_WIDTH // c
    nkb = g // gl
    eye = jnp.eye(gl, dtype=F32)

    def in_map(bb):
        m = bb.transpose(0, 2, 1).reshape(nkb, gl, c, p)
        return jnp.einsum("kgcp,gh->kgchp", m, eye).reshape(nkb, gl * c, gl * p).astype(BF16)

    def out_map(cm):
        m = cm.astype(F32).transpose(0, 2, 1).reshape(nkb, gl, p, c)
        return jnp.einsum("kgpc,gh->kgphc", m, eye).reshape(nkb, gl * p, gl * c).astype(BF16)

    ar, ai = a_re.reshape(-1), a_im.reshape(-1)

    def cmul(x, y):
        return (x[0] * y[0] - x[1] * y[1], x[0] * y[1] + x[1] * y[0])

    a1 = (ar, ai)
    pw = [a1]
    for _ in range(run - 1):
        pw.append(cmul(pw[-1], a1))
    a_pow = jnp.stack([jnp.stack([q[0] for q in pw]), jnp.stack([q[1] for q in pw])])
    hop = [pw[run - 1]]
    for _ in range(2):
        hop.append(cmul(hop[-1], hop[-1]))
    rows = jnp.arange(SUBLANES)[:, None]
    hops = jnp.stack([jnp.stack([jnp.where(rows >= (1 << k), hop[k][0][None, :], 0.0),
                                 jnp.where(rows >= (1 << k), hop[k][1][None, :], 0.0)])
                      for k in range(3)])
    a_one = jnp.stack([ar, ai]).reshape(2, 1, -1)
    return in_map(bb_re), in_map(bb_im), out_map(c_re), out_map(c_im), hops, a_pow, a_one


def _glu_out(y, wg_ref, bg_ref):
    z = _dot(y.astype(BF16), wg_ref[...]) + bg_ref[...]
    d = z.shape[1] // 2
    return z[:, :d] * jax.nn.sigmoid(z[:, d:])


def _s5_prompt_body(x_ref, g_ref, sh_ref, sc_ref, gt_ref, bbr_ref, bbi_ref, cr_ref, ci_ref, ah_ref,
                    ap_ref, a_ref, perm_ref, unperm_ref, dk_ref, wg_ref, bg_ref, o_ref, fre_ref, fim_ref,
                    y_scr, sr_scr, si_scr, st_scr, *, tl):
    i = pl.program_id(1)
    nkb, kin, cw = bbr_ref.shape
    sub = SUBLANES
    run = tl // sub
    last = sub - 1

    @pl.when(i == 0)
    def _():
        st_scr[...] = jnp.zeros_like(st_scr)

    x = x_ref[...]
    h = _modnorm(x, g_ref[...], sh_ref[...], sc_ref[...])
    perm = perm_ref[...]
    hb = _dot(perm, h.astype(BF16)).astype(BF16)
    for kb in range(nkb):
        cols = slice(kb * cw, (kb + 1) * cw)
        hk = hb[:, kb * kin:(kb + 1) * kin]
        sr_scr[...] = _dot(hk, bbr_ref[kb]).reshape(run, sub, cw)
        si_scr[...] = _dot(hk, bbi_ref[kb]).reshape(run, sub, cw)
        a_r = a_ref[0, :, cols]
        a_i = a_ref[1, :, cols]
        s_r = sr_scr[0]
        s_i = si_scr[0]
        for p in range(1, run):
            s_r, s_i = sr_scr[p] + (a_r * s_r - a_i * s_i), si_scr[p] + (a_r * s_i + a_i * s_r)
            sr_scr[p] = s_r
            si_scr[p] = s_i
        row = lax.broadcasted_iota(jnp.int32, (sub, cw), 0)
        e_r = jnp.where(row == 0, st_scr[0, :, cols], pltpu.roll(s_r, 1, 0))
        e_i = jnp.where(row == 0, st_scr[1, :, cols], pltpu.roll(s_i, 1, 0))
        for k in range(3):
            h_r = ah_ref[k, 0, :, cols]
            h_i = ah_ref[k, 1, :, cols]
            p_r = pltpu.roll(e_r, 1 << k, 0)
            p_i = pltpu.roll(e_i, 1 << k, 0)
            e_r, e_i = e_r + h_r * p_r - h_i * p_i, e_i + h_r * p_i + h_i * p_r
        n_r = ap_ref[0, run - 1:run, cols]
        n_i = ap_ref[1, run - 1:run, cols]
        st_scr[0, :, cols] = s_r[last:] + (n_r * e_r[last:] - n_i * e_i[last:])
        st_scr[1, :, cols] = s_i[last:] + (n_r * e_i[last:] + n_i * e_r[last:])
        for p in range(run):
            w_r = ap_ref[0, p:p + 1, cols]
            w_i = ap_ref[1, p:p + 1, cols]
            sr_scr[p] += w_r * e_r - w_i * e_i
            si_scr[p] += w_r * e_i + w_i * e_r
        s_re = sr_scr[...].reshape(tl, cw).astype(BF16)
        s_im = si_scr[...].reshape(tl, cw).astype(BF16)
        y_scr[:, kb * kin:(kb + 1) * kin] = _dot(s_re, cr_ref[kb]) - _dot(s_im, ci_ref[kb])
    y = y_scr[...]
    y1 = y.astype(BF16)
    y2 = (y - y1.astype(F32)).astype(BF16)
    y3 = ((y - y1.astype(F32)) - y2.astype(F32)).astype(BF16)
    unperm = unperm_ref[...]
    y = (_dot(unperm, y1) + _dot(unperm, y2)) + _dot(unperm, y3)
    y = y + dk_ref[...] * h
    o_ref[...] = x + gt_ref[...] * _glu_out(y, wg_ref, bg_ref)
    fre_ref[...] = st_scr[0]
    fim_ref[...] = st_scr[1]


def _s5_prompt_call(x, gain4, mod5, consts, d_skip, w_glu_bf, b_glu, layer, nseq, tl):
    t, d = x.shape
    ni = (t // nseq) // tl
    bbr, bbi, cr, ci, hops, a_pow, a_one = consts
    assert a_pow.shape[1] == tl // SUBLANES, "constants were built for another tile length"
    nkb, kin, cw = bbr.shape
    nch = nkb * cw
    sh, sc, gt = _mod_specs(mod5, layer, 1, ni)

    def mspec(s):
        return pl.BlockSpec(s.block_shape, lambda b, i, _f=s.index_map: _f(b * ni + i))

    def whole(a):
        nd = a.ndim
        return pl.BlockSpec(a.shape, lambda b, i: (0,) * nd, pipeline_mode=pl.Buffered(1))

    dk = d_skip.astype(F32).reshape(1, d)
    bg = b_glu.astype(F32).reshape(1, -1)
    run = tl // SUBLANES
    src = jnp.arange(tl)
    src = (src % SUBLANES) * run + src // SUBLANES
    perm = (src[:, None] == jnp.arange(tl)[None, :]).astype(BF16)
    unperm = perm.T
    tok = lambda b, i: (b * ni + i, 0)
    x_out, fre, fim = pl.pallas_call(
        functools.partial(_s5_prompt_body, tl=tl),
        grid=(nseq, ni),
        in_specs=[pl.BlockSpec((tl, d), tok),
                  pl.BlockSpec((None, None, 1, d), lambda b, i: (layer, 1, 0, 0)),
                  mspec(sh), mspec(sc), mspec(gt),
                  whole(bbr), whole(bbi), whole(cr), whole(ci), whole(hops), whole(a_pow), whole(a_one),
                  whole(perm), whole(unperm), whole(dk), whole(w_glu_bf), whole(bg)],
        out_specs=[pl.BlockSpec((tl, d), tok),
                   pl.BlockSpec((None, 1, nch), lambda b, i: (b, 0, 0)),
                   pl.BlockSpec((None, 1, nch), lambda b, i: (b, 0, 0))],
        out_shape=[jax.ShapeDtypeStruct((t, d), F32),
                   jax.ShapeDtypeStruct((nseq, 1, nch), F32),
                   jax.ShapeDtypeStruct((nseq, 1, nch), F32)],
        scratch_shapes=[pltpu.VMEM((tl, d), F32),
                        pltpu.VMEM((tl // SUBLANES, SUBLANES, cw), F32),
                        pltpu.VMEM((tl // SUBLANES, SUBLANES, cw), F32),
                        pltpu.VMEM((2, 1, nch), F32)],
        **_call_options("s5_mixer_prompt", 2),
    )(x, gain4, mod5, mod5, mod5, bbr, bbi, cr, ci, hops, a_pow, a_one, perm, unperm, dk, w_glu_bf, bg)
    return x_out, fre.reshape(nseq, nch), fim.reshape(nseq, nch)


def _s5_step_body(x_ref, g_ref, sh_ref, sc_ref, gt_ref, pre_ref, pim_ref, bbr_ref, bbi_ref, cr_ref,
                  ci_ref, a_ref, dk_ref, wg_ref, bg_ref, o_ref, nre_ref, nim_ref, y_scr):
    nkb, kin, cw = bbr_ref.shape
    x = x_ref[...]
    h = _modnorm(x, g_ref[...], sh_ref[...], sc_ref[...])
    hb = h.astype(BF16)
    for kb in range(nkb):
        cols = slice(kb * cw, (kb + 1) * cw)
        hk = hb[:, kb * kin:(kb + 1) * kin]
        ar = a_ref[0, :, cols]
        ai = a_ref[1, :, cols]
        pr = pre_ref[:, cols]
        pi = pim_ref[:, cols]
        s_r = _dot(hk, bbr_ref[kb]) + (ar * pr - ai * pi)
        s_i = _dot(hk, bbi_ref[kb]) + (ar * pi + ai * pr)
        nre_ref[:, cols] = s_r
        nim_ref[:, cols] = s_i
        y_scr[:, kb * kin:(kb + 1) * kin] = (_dot(s_r.astype(BF16), cr_ref[kb])
                                             - _dot(s_i.astype(BF16), ci_ref[kb]))
    y = y_scr[...] + dk_ref[...] * h
    o_ref[...] = x + gt_ref[...] * _glu_out(y, wg_ref, bg_ref)


def _s5_step_call(x, gain4, mod5, prev_re, prev_im, consts, d_skip, w_glu_bf, b_glu, layer):
    t, d = x.shape
    bbr, bbi, cr, ci, _, _, a_one = consts
    sh, sc, gt = _mod_specs(mod5, layer, 1, 1)

    def whole(a):
        nd = a.ndim
        return pl.BlockSpec(a.shape, lambda i: (0,) * nd)

    dk = d_skip.astype(F32).reshape(1, d)
    bg = b_glu.astype(F32).reshape(1, -1)
    return pl.pallas_call(
        _s5_step_body,
        grid=(1,),
        in_specs=[whole(x), pl.BlockSpec((None, None, 1, d), lambda i: (layer, 1, 0, 0)),
                  sh, sc, gt, whole(prev_re), whole(prev_im),
                  whole(bbr), whole(bbi), whole(cr), whole(ci), whole(a_one),
                  whole(dk), whole(w_glu_bf), whole(bg)],
        out_specs=[whole(x), whole(prev_re), whole(prev_im)],
        out_shape=[jax.ShapeDtypeStruct((t, d), F32), jax.ShapeDtypeStruct(prev_re.shape, F32),
                   jax.ShapeDtypeStruct(prev_im.shape, F32)],
        scratch_shapes=[pltpu.VMEM((t, d), F32)],
        **_call_options("s5_mixer_step", 1),
    )(x, gain4, mod5, mod5, mod5, prev_re, prev_im, bbr, bbi, cr, ci, a_one, dk, w_glu_bf, bg)


def _tile(n, want):
    t = min(n, want)
    while n % t:
        t //= 2
    return t


def kernel(x_prompt, x_sample, state_conv, cache_k, cache_v, state_ssm_re, state_ssm_im, page_table, c_prompt, c_sample, ln_gain, ada_w, ada_b, ffn_w13, ffn_w2, conv_w_in, conv_w, conv_w_out, attn_w_qkv, attn_q_gain, attn_k_gain, attn_logit_bias, attn_w_o, ssm_lambda_re, ssm_lambda_im, ssm_log_dt, ssm_b_re, ssm_b_im, ssm_c_re, ssm_c_im, ssm_d, ssm_w_glu, ssm_b_glu):
    bp, seq, d = x_prompt.shape
    bs, seq_s, _ = x_sample.shape
    assert seq_s == 1, "the sample trunk handles one new token per sequence"
    depth = ln_gain.shape[0]
    n_heads, head_dim = cache_k.shape[3], cache_k.shape[4]
    page = cache_k.shape[2]
    width = conv_w.shape[1]
    n_state = ssm_lambda_re.shape[1] * ssm_lambda_re.shape[2]

    rows_p = -(-bp // SUBLANES) * SUBLANES
    c_all = jnp.concatenate([c_prompt, jnp.zeros((rows_p - bp, d), F32), c_sample], axis=0)
    mod_p, mod_s = _ada_call(c_all, ada_w, ada_b, rows_p)
    mod_p = mod_p.reshape(depth, N_SUB * 3, rows_p, 1, d)
    mod_s = mod_s.reshape(depth, N_SUB * 3, 1, bs, d)
    gain4 = ln_gain.reshape(depth, N_SUB, 1, d)

    xp = x_prompt.reshape(bp * seq, d)
    xs = x_sample.reshape(bs, d)
    tm_p = _tile(seq, TOKEN_TILE)
    tps = seq // tm_p

    outs = dict(pc=[], pk=[], pv=[], pr=[], pi=[], sc=[], sk=[], sv=[], sr=[], si=[])
    for i in range(depth):
        kind, j = i % N_MIXERS, i // N_MIXERS
        prompt_proj = None
        xs, *w_bf = _ffn_cast_call(xs, gain4, mod_s, ffn_w13, ffn_w2, i, 0, 0)
        xp = _ffn_call(xp, gain4, mod_p, *w_bf, i, 0, tm_p, tps)
        if kind == 0:
            w_in = conv_w_in[j].astype(BF16)
            w_out = conv_w_out[j].astype(BF16)
            prev_p = jnp.zeros((bp, width - 1, d), F32)
            xp, st = _conv_prompt_call(xp, gain4, mod_p, prev_p, w_in, conv_w[j], w_out, i, bp, tm_p)
            outs["pc"].append(st)
            xs, st = _conv_step_call(xs, gain4, mod_s, state_conv[j].reshape(bs, (width - 1) * d),
                                     w_in, conv_w[j], w_out, i)
            outs["sc"].append(st.reshape(bs, width - 1, d))
        elif kind == 1:
            w_qkv = attn_w_qkv[j].astype(BF16)
            w_o = attn_w_o[j].astype(BF16)
            qs, kb, vb, k_t, v_t = _qkv_call(xp, gain4, mod_p, w_qkv, attn_q_gain[j], attn_k_gain[j], i,
                                             bp, tm_p, head_dim, decode=False)
            outs["pk"].append(k_t.reshape(bp, n_heads, head_dim, seq).transpose(0, 3, 1, 2))
            outs["pv"].append(v_t.reshape(bp, n_heads, head_dim, seq).transpose(0, 3, 1, 2))
            o = _sb_prompt_call(qs, kb, vb, attn_logit_bias[j], bp, head_dim, _tile(seq, ATTN_Q_TILE),
                                _tile(seq, ATTN_K_TILE))
            prompt_proj = (o, w_o)
            qs_t, k_t, v_t = _qkv_call(xs, gain4, mod_s, w_qkv, attn_q_gain[j], attn_k_gain[j], i,
                                       1, bs, head_dim, decode=True)
            k_t, v_t = k_t[0], v_t[0]
            outs["sk"].append(k_t.reshape(n_heads, head_dim, bs, 1).transpose(2, 3, 0, 1))
            outs["sv"].append(v_t.reshape(n_heads, head_dim, bs, 1).transpose(2, 3, 0, 1))
            o = _sb_decode_call(qs_t, k_t, v_t, cache_k[j].transpose(0, 2, 3, 1),
                                cache_v[j].transpose(0, 2, 3, 1), page_table, attn_logit_bias[j])
            xs = _proj_res_call(xs, o.reshape(bs, d), mod_s, w_o, i, bs, 1)
        else:
            tl = _tile(seq, S5_TILE)
            consts = _s5_constants(ssm_lambda_re[j], ssm_lambda_im[j], ssm_log_dt[j], ssm_b_re[j],
                                   ssm_b_im[j], ssm_c_re[j], ssm_c_im[j], tl // SUBLANES)
            w_glu = ssm_w_glu[j].astype(BF16)
            xp, fre, fim = _s5_prompt_call(xp, gain4, mod_p, consts, ssm_d[j], w_glu, ssm_b_glu[j], i,
                                           bp, tl)
            outs["pr"].append(fre.reshape(bp, -1, ssm_lambda_re.shape[2]))
            outs["pi"].append(fim.reshape(bp, -1, ssm_lambda_re.shape[2]))
            xs, nre, nim = _s5_step_call(xs, gain4, mod_s, state_ssm_re[j].reshape(bs, n_state),
                                         state_ssm_im[j].reshape(bs, n_state), consts, ssm_d[j], w_glu,
                                         ssm_b_glu[j], i)
            outs["sr"].append(nre.reshape(state_ssm_re.shape[1:]))
            outs["si"].append(nim.reshape(state_ssm_im.shape[1:]))
        xs, *w_bf = _ffn_cast_call(xs, gain4, mod_s, ffn_w13, ffn_w2, i, 1, 2)
        xp = _ffn_call(xp, gain4, mod_p, *w_bf, i, 2, tm_p, tps, mixer_proj=prompt_proj)

    st = {k: jnp.stack(v) for k, v in outs.items()}
    return (xp.reshape(bp, seq, d), xs.reshape(bs, 1, d), st["pc"], st["pk"], st["pv"], st["pr"],
            st["pi"], st["sc"], st["sk"], st["sv"], st["sr"], st["si"])
```
